```python
import numpy as np
import jax
import jax.numpy as jnp
from jax import lax

D_MODEL = 1024
BATCH = 4
SEQ = 8192
DEPTH = 1

D_CONV = 512
CONV_K = 31
N_HEADS = 16
N_KV_GROUPS = 4
HEADS_PER_GROUP = N_HEADS // N_KV_GROUPS
HEAD_DIM = 64
CMP_BLOCK = 32
CMP_STRIDE = 16
CMP_HIDDEN = 2 * HEAD_DIM
SEL_BLOCK = 64
SEL_TOPK = 16
WINDOW = 512
Q_CHUNK = 64
FORCED_SCORE = 1e4
N_EXPERTS = 256
TOP_K = 8
N_GROUPS = 8
TOPK_GROUPS = 4
D_EXPERT = 256
D_SHARED = 256
ROUTED_SCALE = 2.5
MOE_BLOCK = 128

EPS = 1e-6
NEG = -1e30

N_Q = N_HEADS * HEAD_DIM
N_KV = 3 * 2 * N_KV_GROUPS * HEAD_DIM
N_NSA_GATE = 3 * N_HEADS
N_IN = 2 * D_CONV + N_Q + N_KV + N_NSA_GATE + 2 * D_MODEL

kernel_name = 'hybrid_conv_nsa_moe_adaln_block'


def rmsnorm(x, g):
    xf = x.astype(jnp.float32)
    y = xf * lax.rsqrt(jnp.mean(xf * xf, axis=-1, keepdims=True) + EPS)
    return (y * g.astype(jnp.float32)).astype(x.dtype)


def layernorm(x, g, b):
    xf = x.astype(jnp.float32)
    mu = jnp.mean(xf, axis=-1, keepdims=True)
    var = jnp.mean(jnp.square(xf - mu), axis=-1, keepdims=True)
    y = (xf - mu) * lax.rsqrt(var + EPS)
    return (y * g.astype(jnp.float32) + b.astype(jnp.float32)).astype(x.dtype)


def modulate(h, shift, scale):
    return h * (1.0 + scale[:, None, :]) + shift[:, None, :]


def masked_softmax(s, mask):
    p = jax.nn.softmax(jnp.where(mask, s, NEG), axis=-1)
    return jnp.where(mask, p, 0.0)


def swiglu(x, w_gate, w_up, w_down):
    return (jax.nn.silu(x @ w_gate) * (x @ w_up)) @ w_down


def conformer_conv(u_glu, conv_w, conv_b, ln_g, ln_b, w_o, b_o):
    a, gt = jnp.split(u_glu, 2, axis=-1)
    u = a * jax.nn.sigmoid(gt)
    u = lax.conv_general_dilated(u, conv_w[:, None, :].astype(u.dtype), window_strides=(1,),
                                 padding=[(CONV_K - 1, 0)], dimension_numbers=('NWC', 'WIO', 'NWC'),
                                 feature_group_count=D_CONV) + conv_b
    u = jax.nn.silu(layernorm(u, ln_g, ln_b))
    return u @ w_o + b_o


def compress_kv(z, pe, w1, w2):
    B, S, G, Dh = z.shape
    n_cmp = (S - CMP_BLOCK) // CMP_STRIDE + 1
    idx = CMP_STRIDE * np.arange(n_cmp)[:, None] + np.arange(CMP_BLOCK)[None, :]
    blk = z[:, idx] + pe[None, None, :, None, :]
    blk = blk.transpose(0, 1, 3, 2, 4).reshape(B, n_cmp, G, CMP_BLOCK * Dh)
    return jax.nn.gelu(blk @ w1) @ w2


def nsa_attention(q, k_c, v_c, k_s, v_s, k_w, v_w, gates):
    B, S, G, HG, Dh = q.shape
    n_cmp = k_c.shape[1]
    n_sel = S // SEL_BLOCK
    n_top = min(SEL_TOPK, n_sel)
    scale = Dh ** -0.5
    cmp_end = jnp.asarray(CMP_STRIDE * np.arange(n_cmp) + CMP_BLOCK - 1, dtype=jnp.int32)
    cs = CMP_STRIDE * np.arange(n_cmp)[:, None]
    ss = SEL_BLOCK * np.arange(n_sel)[None, :]
    overlap = jnp.asarray(((cs <= ss + SEL_BLOCK - 1) & (cs + CMP_BLOCK - 1 >= ss)).astype(np.float32))
    ks_blk = k_s.reshape(B, n_sel, SEL_BLOCK, G, Dh).transpose(0, 3, 1, 2, 4)
    vs_blk = v_s.reshape(B, n_sel, SEL_BLOCK, G, Dh).transpose(0, 3, 1, 2, 4)
    kw_pad = jnp.pad(k_w, ((0, 0), (WINDOW, 0), (0, 0), (0, 0)))
    vw_pad = jnp.pad(v_w, ((0, 0), (WINDOW, 0), (0, 0), (0, 0)))
    b_ix = jnp.arange(B)[:, None, None, None]
    g_ix = jnp.arange(G)[None, :, None, None]
    sel_ids = jnp.arange(n_sel)

    def chunk(ci):
        t0 = ci * Q_CHUNK
        tq = t0 + jnp.arange(Q_CHUNK)
        qc = lax.dynamic_slice_in_dim(q, t0, Q_CHUNK, axis=1)
        gc = lax.dynamic_slice_in_dim(gates, t0, Q_CHUNK, axis=1)
        s = jnp.einsum('bqghd,bngd->bghqn', qc, k_c).astype(jnp.float32) * scale
        p_c = masked_softmax(s, cmp_end[None, :] <= tq[:, None])
        o_c = jnp.einsum('bghqn,bngd->bqghd', p_c.astype(v_c.dtype), v_c)
        imp = jnp.einsum('bgqn,nj->bgqj', p_c.sum(axis=2), overlap)
        jq = tq // SEL_BLOCK
        forced = (sel_ids[None, :] == 0) | (sel_ids[None, :] == jq[:, None]) | (sel_ids[None, :] == jq[:, None] - 1)
        imp = jnp.where(forced, FORCED_SCORE, jnp.where(sel_ids[None, :] <= jq[:, None], imp, -1.0))
        _, sel = lax.top_k(imp, n_top)
        k_sel = ks_blk[b_ix, g_ix, sel].reshape(B, G, Q_CHUNK, n_top * SEL_BLOCK, Dh)
        v_sel = vs_blk[b_ix, g_ix, sel].reshape(B, G, Q_CHUNK, n_top * SEL_BLOCK, Dh)
        pos = (sel[..., None] * SEL_BLOCK + jnp.arange(SEL_BLOCK)).reshape(B, G, Q_CHUNK, n_top * SEL_BLOCK)
        s = jnp.einsum('bqghd,bgqkd->bghqk', qc, k_sel).astype(jnp.float32) * scale
        p_s = masked_softmax(s, (pos <= tq[None, None, :, None])[:, :, None])
        o_s = jnp.einsum('bghqk,bgqkd->bqghd', p_s.astype(v_sel.dtype), v_sel)
        kwc = lax.dynamic_slice_in_dim(kw_pad, t0, WINDOW + Q_CHUNK, axis=1)
        vwc = lax.dynamic_slice_in_dim(vw_pad, t0, WINDOW + Q_CHUNK, axis=1)
        kpos = t0 - WINDOW + jnp.arange(WINDOW + Q_CHUNK)
        wmask = (kpos[None, :] <= tq[:, None]) & (kpos[None, :] > tq[:, None] - WINDOW) & (kpos[None, :] >= 0)
        s = jnp.einsum('bqghd,bkgd->bghqk', qc, kwc).astype(jnp.float32) * scale
        p_w = masked_softmax(s, wmask)
        o_w = jnp.einsum('bghqk,bkgd->bqghd', p_w.astype(vwc.dtype), vwc)
        return gc[..., 0:1] * o_c + gc[..., 1:2] * o_s + gc[..., 2:3] * o_w

    o = lax.map(chunk, jnp.arange(S // Q_CHUNK))
    return o.transpose(1, 0, 2, 3, 4, 5).reshape(B, S, G * HG * Dh)


def moe_route(xf, w_router, router_bias):
    T = xf.shape[0]
    scores = jax.nn.sigmoid((xf @ w_router).astype(jnp.float32))
    sfc = scores + router_bias.astype(jnp.float32)
    grp = sfc.reshape(T, N_GROUPS, N_EXPERTS // N_GROUPS)
    grp_score = lax.top_k(grp, 2)[0].sum(axis=-1)
    _, gidx = lax.top_k(grp_score, TOPK_GROUPS)
    gmask = jax.nn.one_hot(gidx, N_GROUPS, dtype=jnp.float32).sum(axis=-2) > 0
    emask = jnp.repeat(gmask, N_EXPERTS // N_GROUPS, axis=-1)
    _, idx = lax.top_k(jnp.where(emask, sfc, NEG), TOP_K)
    w = jnp.take_along_axis(scores, idx, axis=-1)
    w = w / jnp.sum(w, axis=-1, keepdims=True) * ROUTED_SCALE
    return idx, w


def moe_routed(xf, idx, wts, w_gate, w_up, w_down):
    T, D = xf.shape
    A = T * TOP_K
    flat_e = idx.reshape(A)
    flat_tok = jnp.arange(A, dtype=jnp.int32) // TOP_K
    flat_w = wts.reshape(A).astype(xf.dtype)
    order = jnp.argsort(flat_e)
    sorted_e = flat_e[order]
    counts = jnp.bincount(flat_e, length=N_EXPERTS)
    padded = (counts + MOE_BLOCK - 1) // MOE_BLOCK * MOE_BLOCK
    pad_end = jnp.cumsum(padded)
    pad_start = pad_end - padded
    start = jnp.cumsum(counts) - counts
    dest = pad_start[sorted_e] + jnp.arange(A) - start[sorted_e]
    n_blocks = -(-A // MOE_BLOCK) + N_EXPERTS
    P = n_blocks * MOE_BLOCK
    buf_tok = jnp.full((P,), T, jnp.int32).at[dest].set(flat_tok[order])
    buf_w = jnp.zeros((P,), xf.dtype).at[dest].set(flat_w[order])
    block_e = jnp.minimum(jnp.searchsorted(pad_end, jnp.arange(n_blocks) * MOE_BLOCK, side='right'), N_EXPERTS - 1)
    x_pad = jnp.concatenate([xf, jnp.zeros((1, D), xf.dtype)], axis=0)

    def body(acc, inp):
        tok, w, e = inp
        xb = x_pad[tok]
        yb = swiglu(xb, w_gate[e], w_up[e], w_down[e]) * w[:, None]
        return acc.at[tok].add(yb), None

    acc, _ = lax.scan(body, jnp.zeros((T + 1, D), xf.dtype),
                      (buf_tok.reshape(n_blocks, MOE_BLOCK), buf_w.reshape(n_blocks, MOE_BLOCK), block_e))
    return acc[:T]


def hybrid_layer(x, c, w_ada, b_ada, norm1_g, w_in, b_in, conv_w, conv_b, conv_ln_g, conv_ln_b,
                 w_conv_out, b_conv_out, cmp_pe_k, cmp_w1_k, cmp_w2_k, cmp_pe_v, cmp_w1_v, cmp_w2_v,
                 w_nsa_out, w_out, norm2_g, w_router, router_bias, w_sh_gate, w_sh_up, w_sh_down,
                 w_gate, w_up, w_down):
    B, S, D = x.shape
    mod = jax.nn.silu(c) @ w_ada + b_ada
    sh1, sc1, g1, sh2, sc2, g2 = jnp.split(mod, 6, axis=-1)
    h = modulate(rmsnorm(x, norm1_g), sh1, sc1)
    proj = h @ w_in + b_in
    offs = [2 * D_CONV, 2 * D_CONV + N_Q, 2 * D_CONV + N_Q + N_KV, 2 * D_CONV + N_Q + N_KV + N_NSA_GATE]
    p_conv, p_q, p_kv, p_ng, p_merge = jnp.split(proj, offs, axis=-1)
    y_a = conformer_conv(p_conv, conv_w, conv_b, conv_ln_g, conv_ln_b, w_conv_out, b_conv_out)
    q = p_q.reshape(B, S, N_KV_GROUPS, HEADS_PER_GROUP, HEAD_DIM)
    kv = p_kv.reshape(B, S, 3, 2, N_KV_GROUPS, HEAD_DIM)
    k_c = compress_kv(kv[:, :, 0, 0], cmp_pe_k, cmp_w1_k, cmp_w2_k)
    v_c = compress_kv(kv[:, :, 0, 1], cmp_pe_v, cmp_w1_v, cmp_w2_v)
    gates = jax.nn.sigmoid(p_ng.reshape(B, S, N_KV_GROUPS, HEADS_PER_GROUP, 3))
    o = nsa_attention(q, k_c, v_c, kv[:, :, 1, 0], kv[:, :, 1, 1], kv[:, :, 2, 0], kv[:, :, 2, 1], gates)
    y_b = o @ w_nsa_out
    gm_a, gm_b = jnp.split(jax.nn.sigmoid(p_merge), 2, axis=-1)
    mix = (gm_a * y_a + gm_b * y_b) @ w_out
    x = x + g1[:, None, :] * mix
    h2 = modulate(rmsnorm(x, norm2_g), sh2, sc2)
    hf = h2.reshape(B * S, D)
    idx, wts = moe_route(hf, w_router, router_bias)
    y = swiglu(hf, w_sh_gate, w_sh_up, w_sh_down) + moe_routed(hf, idx, wts, w_gate, w_up, w_down)
    return x + g2[:, None, :] * y.reshape(B, S, D)


def setup_inputs(seed: int = 0) -> dict:
    key = jax.random.key(seed)
    ks = jax.random.split(key, 40)
    L = DEPTH

    def nrm(k, shape, s):
        return jax.random.normal(k, shape, jnp.float32) * s

    return {
        'x': nrm(ks[0], (BATCH, SEQ, D_MODEL), 1.0),
        'c': nrm(ks[1], (BATCH, D_MODEL), 1.0),
        'w_ada': nrm(ks[2], (L, D_MODEL, 6 * D_MODEL), 0.5 * D_MODEL ** -0.5),
        'b_ada': nrm(ks[3], (L, 6 * D_MODEL), 0.02),
        'norm1_g': 1.0 + nrm(ks[4], (L, D_MODEL), 0.02),
        'w_in': nrm(ks[5], (L, D_MODEL, N_IN), D_MODEL ** -0.5),
        'b_in': nrm(ks[6], (L, N_IN), 0.02),
        'conv_w': nrm(ks[7], (L, CONV_K, D_CONV), CONV_K ** -0.5),
        'conv_b': nrm(ks[8], (L, D_CONV), 0.02),
        'conv_ln_g': 1.0 + nrm(ks[9], (L, D_CONV), 0.02),
        'conv_ln_b': nrm(ks[10], (L, D_CONV), 0.02),
        'w_conv_out': nrm(ks[11], (L, D_CONV, D_MODEL), D_CONV ** -0.5),
        'b_conv_out': nrm(ks[12], (L, D_MODEL), 0.02),
        'cmp_pe_k': nrm(ks[13], (L, CMP_BLOCK, HEAD_DIM), 0.02),
        'cmp_w1_k': nrm(ks[14], (L, CMP_BLOCK * HEAD_DIM, CMP_HIDDEN), (CMP_BLOCK * HEAD_DIM) ** -0.5),
        'cmp_w2_k': nrm(ks[15], (L, CMP_HIDDEN, HEAD_DIM), CMP_HIDDEN ** -0.5),
        'cmp_pe_v': nrm(ks[16], (L, CMP_BLOCK, HEAD_DIM), 0.02),
        'cmp_w1_v': nrm(ks[17], (L, CMP_BLOCK * HEAD_DIM, CMP_HIDDEN), (CMP_BLOCK * HEAD_DIM) ** -0.5),
        'cmp_w2_v': nrm(ks[18], (L, CMP_HIDDEN, HEAD_DIM), CMP_HIDDEN ** -0.5),
        'w_nsa_out': nrm(ks[19], (L, N_Q, D_MODEL), N_Q ** -0.5),
        'w_out': nrm(ks[20], (L, D_MODEL, D_MODEL), D_MODEL ** -0.5),
        'norm2_g': 1.0 + nrm(ks[21], (L, D_MODEL), 0.02),
        'w_router': nrm(ks[22], (L, D_MODEL, N_EXPERTS), D_MODEL ** -0.5),
        'router_bias': nrm(ks[23], (L, N_EXPERTS), 0.01),
        'w_sh_gate': nrm(ks[24], (L, D_MODEL, D_SHARED), D_MODEL ** -0.5),
        'w_sh_up': nrm(ks[25], (L, D_MODEL, D_SHARED), D_MODEL ** -0.5),
        'w_sh_down': nrm(ks[26], (L, D_SHARED, D_MODEL), D_SHARED ** -0.5),
        'w_gate': nrm(ks[27], (L, N_EXPERTS, D_MODEL, D_EXPERT), D_MODEL ** -0.5),
        'w_up': nrm(ks[28], (L, N_EXPERTS, D_MODEL, D_EXPERT), D_MODEL ** -0.5),
        'w_down': nrm(ks[29], (L, N_EXPERTS, D_EXPERT, D_MODEL), D_EXPERT ** -0.5),
        'final_g': 1.0 + nrm(ks[30], (D_MODEL,), 0.02),
    }


def reference(x, c, w_ada, b_ada, norm1_g, w_in, b_in, conv_w, conv_b, conv_ln_g, conv_ln_b,
              w_conv_out, b_conv_out, cmp_pe_k, cmp_w1_k, cmp_w2_k, cmp_pe_v, cmp_w1_v, cmp_w2_v,
              w_nsa_out, w_out, norm2_g, w_router, router_bias, w_sh_gate, w_sh_up, w_sh_down,
              w_gate, w_up, w_down, final_g):
    for l in range(DEPTH):
        x = hybrid_layer(x, c, w_ada[l], b_ada[l], norm1_g[l], w_in[l], b_in[l], conv_w[l], conv_b[l],
                         conv_ln_g[l], conv_ln_b[l], w_conv_out[l], b_conv_out[l], cmp_pe_k[l],
                         cmp_w1_k[l], cmp_w2_k[l], cmp_pe_v[l], cmp_w1_v[l], cmp_w2_v[l], w_nsa_out[l],
                         w_out[l], norm2_g[l], w_router[l], router_bias[l], w_sh_gate[l], w_sh_up[l],
                         w_sh_down[l], w_gate[l], w_up[l], w_down[l])
    return rmsnorm(x, final_g)
```

```python
import functools

import jax
import jax.numpy as jnp
from jax import lax
from jax.experimental import pallas as pl
from jax.experimental.pallas import tpu as pltpu

BF16 = jnp.bfloat16
F32 = jnp.float32
I32 = jnp.int32

D_CONV = 512
CONV_K = 31
N_HEADS = 16
N_KV_GROUPS = 4
HEADS_PER_GROUP = 4
HEAD_DIM = 64
CMP_BLOCK = 32
CMP_STRIDE = 16
SEL_BLOCK = 64
SEL_TOPK = 16
WINDOW = 512
FORCED_SCORE = 1e4
N_EXPERTS = 256
TOP_K = 8
N_GROUPS = 8
TOPK_GROUPS = 4
ROUTED_SCALE = 2.5
EPS = 1e-6
NEG = -1e30
MASK_BIAS = -1e9
LANES = 128
HALO = 32
ATT_TILE = 256
EXPERT_BLOCK = 256
COMBINE_TILE = 256
VMEM_LIMIT = 56 * 1024 * 1024


def _params(*sem):
    return pltpu.CompilerParams(dimension_semantics=sem, vmem_limit_bytes=VMEM_LIMIT)


def _dot(a, b):
    return jnp.dot(a, b, preferred_element_type=F32)


def _dot_nt(a, b):
    return lax.dot_general(a, b, (((1,), (1,)), ((), ())), preferred_element_type=F32)


def _split_bf16(x):
    hi = x.astype(BF16)
    lo = (x - hi.astype(F32)).astype(BF16)
    return hi, lo


def _ada_kernel(c_ref, w_ref, b_ref, o_ref):
    c = c_ref[...]
    a = c * jax.nn.sigmoid(c)
    o_ref[...] = _dot(a.astype(BF16), w_ref[...].astype(BF16)) + b_ref[...]


def ada_modulation(c, w_ada, b_ada):
    B, D = c.shape
    N = w_ada.shape[1]
    rows = 8
    c8 = jnp.zeros((rows, D), F32).at[:B].set(c)
    tn = 1024
    out = pl.pallas_call(
        _ada_kernel,
        grid=(N // tn,),
        in_specs=[pl.BlockSpec((rows, D), lambda j: (0, 0)),
                  pl.BlockSpec((D, tn), lambda j: (0, j)),
                  pl.BlockSpec((1, tn), lambda j: (0, j))],
        out_specs=pl.BlockSpec((rows, tn), lambda j: (0, j)),
        out_shape=jax.ShapeDtypeStruct((rows, N), F32),
        compiler_params=_params("arbitrary"),
        name="ada",
    )(c8, w_ada, b_ada.reshape(1, N))
    return out[:B]


def _modulated_rmsnorm(x, g, sc, sh):
    y = x * lax.rsqrt(jnp.mean(x * x, axis=-1, keepdims=True) + EPS)
    return (y * g) * (1.0 + sc) + sh


def _in_proj_kernel(x_ref, g_ref, sc_ref, sh_ref, w_ref, b_ref, o_ref, h_scr):
    @pl.when(pl.program_id(1) == 0)
    def _():
        h = _modulated_rmsnorm(x_ref[...], g_ref[...], sc_ref[...], sh_ref[...])
        h_scr[...] = h.astype(BF16)

    o_ref[...] = (_dot(h_scr[...], w_ref[...]) + b_ref[...]).astype(BF16)


def in_proj(x2, norm_g, sc, sh, w, b, seq, tm=512, tn=1152):
    T, D = x2.shape
    NP = w.shape[1]
    per_b = seq // tm
    return pl.pallas_call(
        _in_proj_kernel,
        grid=(T // tm, NP // tn),
        in_specs=[pl.BlockSpec((tm, D), lambda i, j: (i, 0)),
                  pl.BlockSpec((1, D), lambda i, j: (0, 0)),
                  pl.BlockSpec((None, 1, D), lambda i, j: (i // per_b, 0, 0)),
                  pl.BlockSpec((None, 1, D), lambda i, j: (i // per_b, 0, 0)),
                  pl.BlockSpec((D, tn), lambda i, j: (0, j)),
                  pl.BlockSpec((1, tn), lambda i, j: (0, j))],
        out_specs=pl.BlockSpec((tm, tn), lambda i, j: (i, j)),
        out_shape=jax.ShapeDtypeStruct((T, NP), BF16),
        scratch_shapes=[pltpu.VMEM((tm, D), BF16)],
        compiler_params=_params("arbitrary", "arbitrary"),
        name="in_proj",
    )(x2, norm_g.reshape(1, D), sc[:, None, :], sh[:, None, :], w, b.reshape(1, NP))


def _conv_kernel(a_ref, gt_ref, pa_ref, pg_ref, mg_ref, cw_ref, cb_ref, lg_ref, lb_ref, wo_ref, bo_ref,
                 o_ref, u_scr, c_scr, *, ts, chunk):
    i = pl.program_id(1)
    a = a_ref[...].astype(F32)
    u_scr[HALO:, :] = a * jax.nn.sigmoid(gt_ref[...].astype(F32))
    pa = pa_ref[...].astype(F32)
    prev = pa * jax.nn.sigmoid(pg_ref[...].astype(F32))
    u_scr[:HALO, :] = jnp.where(i > 0, prev, 0.0)
    off = HALO - (CONV_K - 1)
    for r0 in range(0, ts, chunk):
        acc = jnp.broadcast_to(cb_ref[...], (chunk, D_CONV))
        for k in range(CONV_K):
            acc = acc + cw_ref[k:k + 1, :] * u_scr[r0 + off + k:r0 + off + k + chunk, :]
        c_scr[r0:r0 + chunk, :] = acc
    v = c_scr[...]
    mu = jnp.mean(v, axis=-1, keepdims=True)
    var = jnp.mean(jnp.square(v - mu), axis=-1, keepdims=True)
    y = (v - mu) * lax.rsqrt(var + EPS) * lg_ref[...] + lb_ref[...]
    y = y * jax.nn.sigmoid(y)
    ya = _dot(y.astype(BF16), wo_ref[...]) + bo_ref[...]
    o_ref[...] = (jax.nn.sigmoid(mg_ref[...].astype(F32)) * ya).astype(BF16)


def conformer_conv(proj, B, S, D, merge_col, conv_w, conv_b, ln_g, ln_b, w_o, b_o, ts=512, chunk=64):
    T = B * S
    nt = S // ts
    hb = ts // HALO
    prev_idx = lambda b, i: (jnp.maximum((b * nt + i) * hb - 1, 0), 0)
    prev_idx_g = lambda b, i: (jnp.maximum((b * nt + i) * hb - 1, 0), 1)
    row = lambda v: v.reshape(1, -1)
    return pl.pallas_call(
        functools.partial(_conv_kernel, ts=ts, chunk=chunk),
        grid=(B, nt),
        in_specs=[pl.BlockSpec((ts, D_CONV), lambda b, i: (b * nt + i, 0)),
                  pl.BlockSpec((ts, D_CONV), lambda b, i: (b * nt + i, 1)),
                  pl.BlockSpec((HALO, D_CONV), prev_idx),
                  pl.BlockSpec((HALO, D_CONV), prev_idx_g),
                  pl.BlockSpec((ts, D), lambda b, i: (b * nt + i, merge_col // D)),
                  pl.BlockSpec((CONV_K, D_CONV), lambda b, i: (0, 0)),
                  pl.BlockSpec((1, D_CONV), lambda b, i: (0, 0)),
                  pl.BlockSpec((1, D_CONV), lambda b, i: (0, 0)),
                  pl.BlockSpec((1, D_CONV), lambda b, i: (0, 0)),
                  pl.BlockSpec((D_CONV, D), lambda b, i: (0, 0)),
                  pl.BlockSpec((1, D), lambda b, i: (0, 0))],
        out_specs=pl.BlockSpec((ts, D), lambda b, i: (b * nt + i, 0)),
        out_shape=jax.ShapeDtypeStruct((T, D), BF16),
        scratch_shapes=[pltpu.VMEM((ts + HALO, D_CONV), F32), pltpu.VMEM((ts, D_CONV), F32)],
        compiler_params=_params("arbitrary", "arbitrary"),
        name="conv",
    )(proj, proj, proj, proj, proj, conv_w, row(conv_b), row(ln_g), row(ln_b), w_o.astype(BF16), row(b_o))


def _compress_kernel(z_ref, pe_ref, w1_ref, w2_ref, o_ref, *, nc):
    half = w1_ref.shape[0] // 2
    z = z_ref[...]
    first = _dot(z, w1_ref[:half, :])
    second = _dot(z, w1_ref[half:, :])
    bias = _dot(pe_ref[...], w1_ref[...])[0:1, :]
    hid = first + pltpu.roll(second, nc - 1, 0) + bias
    act = jax.nn.gelu(hid)
    out = _dot(act.astype(BF16), w2_ref[...])
    rows = lax.broadcasted_iota(I32, out.shape, 0)
    o_ref[...] = jnp.where(rows < nc - 1, out, 0.0).astype(BF16)


def compress_kv(zc, pe, w1, w2):
    B, _, G, NC, W = zc.shape
    H = w1.shape[-1]
    pe8 = jnp.broadcast_to(pe.reshape(2, 1, 2 * W), (2, 8, 2 * W)).astype(BF16)
    return pl.pallas_call(
        functools.partial(_compress_kernel, nc=NC),
        grid=(B, 2, G),
        in_specs=[pl.BlockSpec((None, None, None, NC, W), lambda b, s, g: (b, s, g, 0, 0)),
                  pl.BlockSpec((None, 8, 2 * W), lambda b, s, g: (s, 0, 0)),
                  pl.BlockSpec((None, 2 * W, H), lambda b, s, g: (s, 0, 0)),
                  pl.BlockSpec((None, H, HEAD_DIM), lambda b, s, g: (s, 0, 0))],
        out_specs=pl.BlockSpec((None, None, None, NC, HEAD_DIM), lambda b, s, g: (b, s, g, 0, 0)),
        out_shape=jax.ShapeDtypeStruct((B, 2, G, NC, HEAD_DIM), BF16),
        compiler_params=_params("arbitrary", "arbitrary", "arbitrary"),
        name="compress",
    )(zc, pe8, w1.astype(BF16), w2.astype(BF16))


def _cmp_attn_kernel(q_ref, kc_ref, vc_ref, ov_ref, oc_ref, sb_ref, *, tq, ns):
    i = pl.program_id(2)
    t0 = i * tq
    nc = kc_ref.shape[0]
    q = q_ref[...]
    kc = kc_ref[...]
    vc = vc_ref[...]
    t_ids = t0 + lax.broadcasted_iota(I32, (tq, nc), 0)
    n_ids = lax.broadcasted_iota(I32, (tq, nc), 1)
    valid = (CMP_STRIDE * n_ids + (CMP_BLOCK - 1)) <= t_ids
    psum = jnp.zeros((tq, nc), F32)
    for h in range(HEADS_PER_GROUP):
        qh = q[:, h * HEAD_DIM:(h + 1) * HEAD_DIM]
        s = _dot_nt(qh, kc) * (HEAD_DIM ** -0.5)
        s = jnp.where(valid, s, NEG)
        m = jnp.max(s, axis=-1, keepdims=True)
        e = jnp.where(valid, jnp.exp(s - m), 0.0)
        den = jnp.sum(e, axis=-1, keepdims=True)
        p = e / jnp.where(den > 0.0, den, 1.0)
        oc_ref[:, h * HEAD_DIM:(h + 1) * HEAD_DIM] = _dot(p.astype(BF16), vc).astype(BF16)
        psum = psum + p
    ps_hi, ps_lo = _split_bf16(psum)
    imp = _dot(ps_hi, ov_ref[...]) + _dot(ps_lo, ov_ref[...])
    lane = lax.broadcasted_iota(I32, (tq, LANES), 1)
    jq = (t0 + lax.broadcasted_iota(I32, (tq, LANES), 0)) >> (SEL_BLOCK.bit_length() - 1)
    forced = (lane == 0) | (lane == jq) | (lane == jq - 1)
    imp = jnp.where(forced, FORCED_SCORE, jnp.where(lane <= jq, imp, -1.0))
    work = jnp.where(lane < ns, imp, -jnp.inf)
    lane_f = lane.astype(F32)
    sel = jnp.zeros((tq, LANES), jnp.bool_)
    for _ in range(SEL_TOPK):
        m = jnp.max(work, axis=-1, keepdims=True)
        first = jnp.min(jnp.where(work == m, lane_f, float(LANES)), axis=-1, keepdims=True)
        pick = lane_f == first
        sel = sel | pick
        work = jnp.where(pick, -jnp.inf, work)
    sb_ref[...] = jnp.where(sel & (lane <= jq), 0.0, MASK_BIAS).astype(BF16)


def cmp_attention(proj, kvc, overlap, B, S, q_col, tq=ATT_TILE):
    G = N_KV_GROUPS
    NC = kvc.shape[3]
    nt = S // tq
    gw = HEADS_PER_GROUP * HEAD_DIM
    ns = S // SEL_BLOCK
    assert SEL_TOPK <= ns <= LANES
    return pl.pallas_call(
        functools.partial(_cmp_attn_kernel, tq=tq, ns=ns),
        grid=(B, G, nt),
        in_specs=[pl.BlockSpec((tq, gw), lambda b, g, i: (b * nt + i, q_col // gw + g)),
                  pl.BlockSpec((None, None, None, NC, HEAD_DIM), lambda b, g, i: (b, 0, g, 0, 0)),
                  pl.BlockSpec((None, None, None, NC, HEAD_DIM), lambda b, g, i: (b, 1, g, 0, 0)),
                  pl.BlockSpec((NC, LANES), lambda b, g, i: (0, 0))],
        out_specs=[pl.BlockSpec((tq, gw), lambda b, g, i: (b * nt + i, g)),
                   pl.BlockSpec((None, None, tq, LANES), lambda b, g, i: (b, g, i, 0))],
        out_shape=[jax.ShapeDtypeStruct((B * S, G * gw), BF16),
                   jax.ShapeDtypeStruct((B, G, S, LANES), BF16)],
        compiler_params=_params("arbitrary", "arbitrary", "arbitrary"),
        name="cmp_attn",
    )(proj, kvc, kvc, overlap)


def _nsa_kernel(q_ref, sb_ref, ksa_ref, vs_ref, kw_ref, vw_ref, oc_ref, ng_ref, o_ref,
                qa_scr, m_scr, l_scr, acc_scr, os_scr, *, tq):
    g = pl.program_id(1)
    i = pl.program_id(2)
    HG = HEADS_PER_GROUP
    R = HG * tq
    q = q_ref[...] * (HEAD_DIM ** -0.5)
    sb = sb_ref[...]
    for h in range(HG):
        qa_scr[h * tq:(h + 1) * tq, :LANES] = sb
        qa_scr[h * tq:(h + 1) * tq, LANES:] = q[:, h * HEAD_DIM:(h + 1) * HEAD_DIM]

    q_loc = lax.broadcasted_iota(I32, (R, tq), 0) & (tq - 1)
    k_loc = lax.broadcasted_iota(I32, (R, tq), 1)

    def reset():
        m_scr[...] = jnp.full(m_scr.shape, NEG, F32)
        l_scr[...] = jnp.zeros(l_scr.shape, F32)
        acc_scr[...] = jnp.zeros(acc_scr.shape, F32)

    def step(s, v):
        m_prev = m_scr[...]
        m_new = jnp.maximum(m_prev, jnp.max(s, axis=-1, keepdims=True))
        alpha = jnp.exp(m_prev - m_new)
        p = jnp.exp(s - m_new)
        l_scr[...] = alpha * l_scr[...] + jnp.sum(p, axis=-1, keepdims=True)
        acc_scr[...] = alpha * acc_scr[...] + _dot(p.astype(BF16), v)
        m_scr[...] = m_new

    def sel_scores(kt):
        start = pl.multiple_of(kt * tq, tq)
        return _dot_nt(qa_scr[...], ksa_ref[pl.ds(start, tq), :]), vs_ref[pl.ds(start, tq), :]

    def win_scores(kt):
        start = pl.multiple_of(kt * tq, tq)
        return _dot_nt(qa_scr[:, LANES:], kw_ref[pl.ds(start, tq), :]), vw_ref[pl.ds(start, tq), :]

    reset()

    def sel_body(kt, carry):
        s, v = sel_scores(kt)
        step(s, v)
        return carry

    lax.fori_loop(0, i, sel_body, 0)
    s, v = sel_scores(i)
    step(jnp.where(k_loc <= q_loc, s, NEG), v)
    os_scr[...] = acc_scr[...] / l_scr[...]

    reset()

    @pl.when(i >= 2)
    def _():
        s, v = win_scores(i - 2)
        step(jnp.where(k_loc > q_loc, s, NEG), v)

    @pl.when(i >= 1)
    def _():
        s, v = win_scores(i - 1)
        step(s, v)

    s, v = win_scores(i)
    step(jnp.where(k_loc <= q_loc, s, NEG), v)
    ow = acc_scr[...] / l_scr[...]

    gates = jax.nn.sigmoid(ng_ref[...].astype(F32))
    oc = oc_ref[...].astype(F32)
    for h in range(HG):
        c0 = (g * HG + h) * 3
        lane = lax.broadcasted_iota(I32, gates.shape, 1)
        gc = jnp.sum(jnp.where(lane == c0, gates, 0.0), axis=-1, keepdims=True)
        gs = jnp.sum(jnp.where(lane == c0 + 1, gates, 0.0), axis=-1, keepdims=True)
        gw = jnp.sum(jnp.where(lane == c0 + 2, gates, 0.0), axis=-1, keepdims=True)
        rows = slice(h * tq, (h + 1) * tq)
        cols = slice(h * HEAD_DIM, (h + 1) * HEAD_DIM)
        o_ref[:, cols] = (gc * oc[:, cols] + gs * os_scr[rows, :] + gw * ow[rows, :]).astype(BF16)


def nsa_attention(proj, selbias, ksa, kvt, oc, B, S, q_col, ng_col, tq=ATT_TILE):
    G = N_KV_GROUPS
    nt = S // tq
    gw = HEADS_PER_GROUP * HEAD_DIM
    assert WINDOW == 2 * tq and tq & (tq - 1) == 0
    R = HEADS_PER_GROUP * tq
    kv_spec = lambda br, kv: pl.BlockSpec((None, None, None, None, S, HEAD_DIM),
                                          lambda b, g, i: (br, kv, b, g, 0, 0))
    return pl.pallas_call(
        functools.partial(_nsa_kernel, tq=tq),
        grid=(B, G, nt),
        in_specs=[pl.BlockSpec((tq, gw), lambda b, g, i: (b * nt + i, q_col // gw + g)),
                  pl.BlockSpec((None, None, tq, LANES), lambda b, g, i: (b, g, i, 0)),
                  pl.BlockSpec((None, None, S, LANES + HEAD_DIM), lambda b, g, i: (b, g, 0, 0)),
                  kv_spec(1, 1), kv_spec(2, 0), kv_spec(2, 1),
                  pl.BlockSpec((tq, gw), lambda b, g, i: (b * nt + i, g)),
                  pl.BlockSpec((tq, LANES), lambda b, g, i: (b * nt + i, ng_col // LANES))],
        out_specs=pl.BlockSpec((tq, gw), lambda b, g, i: (b * nt + i, g)),
        out_shape=jax.ShapeDtypeStruct((B * S, G * gw), BF16),
        scratch_shapes=[pltpu.VMEM((R, LANES + HEAD_DIM), BF16),
                        pltpu.VMEM((R, 1), F32), pltpu.VMEM((R, 1), F32),
                        pltpu.VMEM((R, HEAD_DIM), F32), pltpu.VMEM((R, HEAD_DIM), F32)],
        compiler_params=_params("arbitrary", "arbitrary", "arbitrary"),
        name="nsa_attn",
    )(proj, selbias, ksa, kvt, kvt, kvt, oc, proj)


def _route(logits, rbias):
    tm, E = logits.shape
    gsz = E // N_GROUPS
    scores = jax.nn.sigmoid(logits)
    sfc = scores + rbias
    lane = lax.broadcasted_iota(I32, (tm, E), 1)
    lane_f = lane.astype(F32)
    grp = lane >> (gsz.bit_length() - 1)
    gscore = []
    for g in range(N_GROUPS):
        v = jnp.where(grp == g, sfc, -jnp.inf)
        m1 = jnp.max(v, axis=-1, keepdims=True)
        cnt = jnp.sum(jnp.where(v == m1, 1.0, 0.0), axis=-1, keepdims=True)
        m2 = jnp.max(jnp.where(v < m1, v, -jnp.inf), axis=-1, keepdims=True)
        gscore.append(m1 + jnp.where(cnt >= 2.0, m1, m2))
    emask = jnp.zeros((tm, E), jnp.bool_)
    for g in range(N_GROUPS):
        rank = jnp.zeros((tm, 1), F32)
        for o in range(N_GROUPS):
            if o == g:
                continue
            ahead = (gscore[o] > gscore[g]) | ((gscore[o] == gscore[g]) & (o < g))
            rank = rank + jnp.where(ahead, 1.0, 0.0)
        emask = emask | ((grp == g) & (rank < float(TOPK_GROUPS)))
    work = jnp.where(emask, sfc, NEG)
    out_lane = lax.broadcasted_iota(I32, (tm, LANES), 1)
    idx_out = jnp.zeros((tm, LANES), F32)
    w_out = jnp.zeros((tm, LANES), F32)
    wsum = jnp.zeros((tm, 1), F32)
    for r in range(TOP_K):
        m = jnp.max(work, axis=-1, keepdims=True)
        first = jnp.min(jnp.where(work == m, lane_f, float(E)), axis=-1, keepdims=True)
        pick = lane_f == first
        w = jnp.sum(jnp.where(pick, scores, 0.0), axis=-1, keepdims=True)
        idx_out = jnp.where(out_lane == r, first, idx_out)
        w_out = jnp.where(out_lane == r, w, w_out)
        wsum = wsum + w
        work = jnp.where(pick, -jnp.inf, work)
    return idx_out.astype(I32), w_out / wsum * ROUTED_SCALE


def _out_proj_kernel(o_ref, za_ref, mb_ref, x_ref, g1_ref, n2_ref, sc_ref, sh_ref, wn_ref, wo_ref,
                     wrh_ref, wrl_ref, rb_ref, x1_ref, h2_ref, idx_ref, wt_ref):
    yb = _dot(o_ref[...], wn_ref[...])
    z = za_ref[...].astype(F32) + jax.nn.sigmoid(mb_ref[...].astype(F32)) * yb
    mix = _dot(z.astype(BF16), wo_ref[...])
    x1 = x_ref[...] + g1_ref[...] * mix
    x1_ref[...] = x1
    h2 = _modulated_rmsnorm(x1, n2_ref[...], sc_ref[...], sh_ref[...])
    h2_ref[...] = h2
    hi, lo = _split_bf16(h2)
    logits = _dot(hi, wrh_ref[...]) + (_dot(lo, wrh_ref[...]) + _dot(hi, wrl_ref[...]))
    idx, wts = _route(logits, rb_ref[...])
    idx_ref[...] = idx
    wt_ref[...] = wts


def out_proj(o, za, proj, x2, g1, norm_g, sc, sh, w_nsa_out, w_out, w_router, router_bias, seq, mb_col, tm=256):
    T, D = x2.shape
    E = w_router.shape[1]
    per_b = seq // tm
    wr_hi, wr_lo = _split_bf16(w_router)
    tok = lambda i: (i, 0)
    const = lambda i: (0, 0)
    per_batch = pl.BlockSpec((None, 1, D), lambda i: (i // per_b, 0, 0))
    return pl.pallas_call(
        _out_proj_kernel,
        grid=(T // tm,),
        in_specs=[pl.BlockSpec((tm, D), tok), pl.BlockSpec((tm, D), tok),
                  pl.BlockSpec((tm, D), lambda i: (i, mb_col // D)),
                  pl.BlockSpec((tm, D), tok), per_batch,
                  pl.BlockSpec((1, D), const), per_batch, per_batch,
                  pl.BlockSpec((D, D), const), pl.BlockSpec((D, D), const),
                  pl.BlockSpec((D, E), const), pl.BlockSpec((D, E), const), pl.BlockSpec((1, E), const)],
        out_specs=[pl.BlockSpec((tm, D), tok), pl.BlockSpec((tm, D), tok),
                   pl.BlockSpec((tm, LANES), tok), pl.BlockSpec((tm, LANES), tok)],
        out_shape=[jax.ShapeDtypeStruct((T, D), F32), jax.ShapeDtypeStruct((T, D), F32),
                   jax.ShapeDtypeStruct((T, LANES), I32), jax.ShapeDtypeStruct((T, LANES), F32)],
        compiler_params=_params("arbitrary"),
        name="out_proj",
    )(o, za, proj, x2, g1[:, None, :], norm_g.reshape(1, D), sc[:, None, :], sh[:, None, :],
      w_nsa_out.astype(BF16), w_out.astype(BF16), wr_hi, wr_lo, router_bias.reshape(1, E))


def _experts_kernel(be_ref, nu_ref, tok_hbm, h_hbm, wg_ref, wu_ref, wd_ref, y_ref,
                    tok_smem, x_buf, tok_sem, row_sem, *, bm):
    n = pl.program_id(0)
    n_used = nu_ref[0]
    slot = n % 2

    def tok_copy(blk, s):
        return pltpu.make_async_copy(tok_hbm.at[blk], tok_smem.at[s], tok_sem.at[s])

    def issue_rows(s):
        def body(r, carry):
            t = tok_smem[s, r]
            pltpu.make_async_copy(h_hbm.at[pl.ds(t, 1), :], x_buf.at[s, pl.ds(r, 1), :], row_sem.at[s]).start()
            return carry
        lax.fori_loop(0, bm, body, 0, unroll=8)

    def wait_rows(s):
        pltpu.make_async_copy(h_hbm.at[pl.ds(0, bm), :], x_buf.at[s], row_sem.at[s]).wait()

    @pl.when(n == 0)
    def _():
        tok_copy(0, 0).start()
        tok_copy(0, 0).wait()
        issue_rows(0)

        @pl.when(n_used > 1)
        def _():
            tok_copy(1, 1).start()

    @pl.when(n + 1 < n_used)
    def _():
        tok_copy(n + 1, 1 - slot).wait()
        issue_rows(1 - slot)

    @pl.when(n + 2 < n_used)
    def _():
        tok_copy(n + 2, slot).start()

    @pl.when(n < n_used)
    def _():
        wait_rows(slot)
        x = x_buf[slot].astype(BF16)
        gate = _dot(x, wg_ref[...])
        up = _dot(x, wu_ref[...])
        hid = gate * jax.nn.sigmoid(gate) * up
        y_ref[...] = _dot(hid.astype(BF16), wd_ref[...])

    @pl.when(n >= n_used)
    def _():
        y_ref[...] = jnp.zeros(y_ref.shape, F32)


def routed_experts(h2, tok_blocks, block_e, n_used, w_gate, w_up, w_down, bm=EXPERT_BLOCK):
    T, D = h2.shape
    nblk = tok_blocks.shape[0]
    E, _, DE = w_gate.shape
    grid_spec = pltpu.PrefetchScalarGridSpec(
        num_scalar_prefetch=2,
        grid=(nblk,),
        in_specs=[pl.BlockSpec(memory_space=pl.ANY),
                  pl.BlockSpec(memory_space=pl.ANY),
                  pl.BlockSpec((None, D, DE), lambda n, be, nu: (be[n], 0, 0)),
                  pl.BlockSpec((None, D, DE), lambda n, be, nu: (be[n], 0, 0)),
                  pl.BlockSpec((None, DE, D), lambda n, be, nu: (be[n], 0, 0))],
        out_specs=pl.BlockSpec((bm, D), lambda n, be, nu: (n, 0)),
        scratch_shapes=[pltpu.SMEM((2, bm), I32), pltpu.VMEM((2, bm, D), F32),
                        pltpu.SemaphoreType.DMA((2,)), pltpu.SemaphoreType.DMA((2,))],
    )
    return pl.pallas_call(
        functools.partial(_experts_kernel, bm=bm),
        grid_spec=grid_spec,
        out_shape=jax.ShapeDtypeStruct((nblk * bm, D), F32),
        compiler_params=_params("arbitrary"),
        name="experts",
    )(block_e, n_used, tok_blocks, h2, w_gate, w_up, w_down)


def _combine_kernel(pos_hbm, y_hbm, h2_ref, x1_ref, wt_ref, g2_ref, fg_ref, wsg_ref, wsu_ref, wsd_ref, o_ref,
                    pos_smem, y_buf, pos_sem, row_sem, *, tj):
    n = pl.program_id(0)
    nsteps = pl.num_programs(0)
    slot = n % 2
    nrow = tj * TOP_K

    def pos_copy(blk, s):
        return pltpu.make_async_copy(pos_hbm.at[blk], pos_smem.at[s], pos_sem.at[s])

    def issue_rows(s):
        def body(a, carry):
            p = pos_smem[s, a]
            pltpu.make_async_copy(y_hbm.at[pl.ds(p, 1), :], y_buf.at[s, pl.ds(a, 1), :], row_sem.at[s]).start()
            return carry
        lax.fori_loop(0, nrow, body, 0, unroll=8)

    def wait_rows(s):
        pltpu.make_async_copy(y_hbm.at[pl.ds(0, nrow), :], y_buf.at[s], row_sem.at[s]).wait()

    @pl.when(n == 0)
    def _():
        pos_copy(0, 0).start()
        pos_copy(0, 0).wait()
        issue_rows(0)

        @pl.when(nsteps > 1)
        def _():
            pos_copy(1, 1).start()

    @pl.when(n + 1 < nsteps)
    def _():
        pos_copy(n + 1, 1 - slot).wait()
        issue_rows(1 - slot)

    @pl.when(n + 2 < nsteps)
    def _():
        pos_copy(n + 2, slot).start()

    h2 = h2_ref[...].astype(BF16)
    gate = _dot(h2, wsg_ref[...])
    up = _dot(h2, wsu_ref[...])
    y = _dot((gate * jax.nn.sigmoid(gate) * up).astype(BF16), wsd_ref[...])
    wait_rows(slot)
    wt = wt_ref[...]
    for k in range(TOP_K):
        y = y + wt[:, k:k + 1] * y_buf[slot, k * tj:(k + 1) * tj, :]
    x2 = x1_ref[...] + g2_ref[...] * y
    o_ref[...] = x2 * lax.rsqrt(jnp.mean(x2 * x2, axis=-1, keepdims=True) + EPS) * fg_ref[...]


def combine(pos_blocks, y_sorted, h2, x1, wts, g2, final_g, w_sh_gate, w_sh_up, w_sh_down, seq, tj=COMBINE_TILE):
    T, D = x1.shape
    DS = w_sh_gate.shape[1]
    per_b = seq // tj
    tok = lambda i: (i, 0)
    const = lambda i: (0, 0)
    return pl.pallas_call(
        functools.partial(_combine_kernel, tj=tj),
        grid=(T // tj,),
        in_specs=[pl.BlockSpec(memory_space=pl.ANY), pl.BlockSpec(memory_space=pl.ANY),
                  pl.BlockSpec((tj, D), tok), pl.BlockSpec((tj, D), tok), pl.BlockSpec((tj, LANES), tok),
                  pl.BlockSpec((None, 1, D), lambda i: (i // per_b, 0, 0)),
                  pl.BlockSpec((1, D), const),
                  pl.BlockSpec((D, DS), const), pl.BlockSpec((D, DS), const), pl.BlockSpec((DS, D), const)],
        out_specs=pl.BlockSpec((tj, D), tok),
        out_shape=jax.ShapeDtypeStruct((T, D), F32),
        scratch_shapes=[pltpu.SMEM((2, tj * TOP_K), I32), pltpu.VMEM((2, tj * TOP_K, D), F32),
                        pltpu.SemaphoreType.DMA((2,)), pltpu.SemaphoreType.DMA((2,))],
        compiler_params=_params("arbitrary"),
        name="combine",
    )(pos_blocks, y_sorted, h2, x1, wts, g2[:, None, :], final_g.reshape(1, D),
      w_sh_gate.astype(BF16), w_sh_up.astype(BF16), w_sh_down.astype(BF16))


def _dispatch_plan(idx, T, bm, tj):
    A = T * TOP_K
    flat_e = idx.reshape(A)
    order = jnp.argsort(flat_e).astype(I32)
    sorted_e = flat_e[order]
    counts = jnp.bincount(flat_e, length=N_EXPERTS).astype(I32)
    padded = (counts + bm - 1) // bm * bm
    pad_end = jnp.cumsum(padded)
    pad_start = pad_end - padded
    start = jnp.cumsum(counts) - counts
    dest = pad_start[sorted_e] + jnp.arange(A, dtype=I32) - start[sorted_e]
    nblk = A // bm + N_EXPERTS
    tok_sorted = jnp.zeros((nblk * bm,), I32).at[dest].set(order // TOP_K)
    pos = jnp.zeros((A,), I32).at[order].set(dest)
    block_e = jnp.minimum(jnp.searchsorted(pad_end, jnp.arange(nblk, dtype=I32) * bm, side='right'),
                          N_EXPERTS - 1).astype(I32)
    n_used = (pad_end[-1] // bm).astype(I32).reshape(1)
    pos_blocks = pos.reshape(T // tj, tj, TOP_K).transpose(0, 2, 1).reshape(T // tj, tj * TOP_K)
    return tok_sorted.reshape(nblk, bm), pos_blocks, block_e, n_used


def _layer(x, c, w_ada, b_ada, norm1_g, w_in, b_in, conv_w, conv_b, conv_ln_g, conv_ln_b, w_conv_out, b_conv_out,
           cmp_pe_k, cmp_w1_k, cmp_w2_k, cmp_pe_v, cmp_w1_v, cmp_w2_v, w_nsa_out, w_out, norm2_g, w_router,
           router_bias, w_sh_gate, w_sh_up, w_sh_down, w_gate, w_up, w_down, final_g):
    B, S, D = x.shape
    T = B * S
    G, dh = N_KV_GROUPS, HEAD_DIM
    n_q = N_HEADS * dh
    n_kv = 3 * 2 * G * dh
    n_ng = 3 * N_HEADS
    mod = ada_modulation(c, w_ada, b_ada)
    sh1, sc1, g1, sh2, sc2, g2 = jnp.split(mod, 6, axis=-1)

    c_q = 2 * D_CONV
    c_kv = c_q + n_q
    c_ng = c_kv + n_kv
    c_mg = c_ng + n_ng
    pad = LANES - n_ng
    w_r = jnp.concatenate([w_in[:, :c_kv], w_in[:, c_mg:], w_in[:, c_kv:c_mg], jnp.zeros((D, pad), F32)], axis=1)
    b_r = jnp.concatenate([b_in[:c_kv], b_in[c_mg:], b_in[c_kv:c_mg], jnp.zeros((pad,), F32)])
    col_merge = c_kv
    col_kv = col_merge + 2 * D
    col_ng = col_kv + n_kv
    x2 = x.reshape(T, D)
    proj = in_proj(x2, norm1_g, sc1, sh1, w_r.astype(BF16), b_r, S)

    za = conformer_conv(proj, B, S, D, col_merge, conv_w, conv_b, conv_ln_g, conv_ln_b, w_conv_out, b_conv_out)

    kv = proj[:, col_kv:col_kv + n_kv].reshape(B, S, 3, 2, G, dh)
    NC = S // CMP_STRIDE
    zc = kv[:, :, 0].reshape(B, NC, CMP_STRIDE, 2, G, dh).transpose(0, 3, 4, 1, 2, 5).reshape(B, 2, G, NC, CMP_STRIDE * dh)
    kvc = compress_kv(zc, jnp.stack([cmp_pe_k, cmp_pe_v]), jnp.stack([cmp_w1_k, cmp_w1_v]),
                      jnp.stack([cmp_w2_k, cmp_w2_v]))

    cs = CMP_STRIDE * jnp.arange(NC)[:, None]
    ss = SEL_BLOCK * jnp.arange(LANES)[None, :]
    overlap = ((cs <= ss + SEL_BLOCK - 1) & (cs + CMP_BLOCK - 1 >= ss) & (jnp.arange(NC)[:, None] < NC - 1)
               & (jnp.arange(LANES)[None, :] < S // SEL_BLOCK)).astype(BF16)
    oc, selbias = cmp_attention(proj, kvc, overlap, B, S, c_q)

    kvt = kv.transpose(2, 3, 0, 4, 1, 5)
    onehot = (jnp.arange(S)[:, None] // SEL_BLOCK == jnp.arange(LANES)[None, :]).astype(BF16)
    ksa = jnp.concatenate([jnp.broadcast_to(onehot, (B, G, S, LANES)), kvt[1, 0]], axis=-1)
    o = nsa_attention(proj, selbias, ksa, kvt, oc, B, S, c_q, col_ng)

    x1, h2, idx, wts = out_proj(o, za, proj, x2, g1, norm2_g, sc2, sh2, w_nsa_out, w_out, w_router, router_bias,
                                S, col_merge + D)

    tok_blocks, pos_blocks, block_e, n_used = _dispatch_plan(idx[:, :TOP_K], T, EXPERT_BLOCK, COMBINE_TILE)
    y_sorted = routed_experts(h2, tok_blocks, block_e, n_used,
                              w_gate.astype(BF16), w_up.astype(BF16), w_down.astype(BF16))
    out = combine(pos_blocks, y_sorted, h2, x1, wts, g2, final_g, w_sh_gate, w_sh_up, w_sh_down, S)
    return out.reshape(B, S, D)


def kernel(x, c, w_ada, b_ada, norm1_g, w_in, b_in, conv_w, conv_b, conv_ln_g, conv_ln_b, w_conv_out, b_conv_out, cmp_pe_k, cmp_w1_k, cmp_w2_k, cmp_pe_v, cmp_w1_v, cmp_w2_v, w_nsa_out, w_out, norm2_g, w_router, router_bias, w_sh_gate, w_sh_up, w_sh_down, w_gate, w_up, w_down, final_g):
    assert w_ada.shape[0] == 1, "single-layer block"
    return _layer(x, c, w_ada[0], b_ada[0], norm1_g[0], w_in[0], b_in[0], conv_w[0], conv_b[0], conv_ln_g[0],
                  conv_ln_b[0], w_conv_out[0], b_conv_out[0], cmp_pe_k[0], cmp_w1_k[0], cmp_w2_k[0], cmp_pe_v[0],
                  cmp_w1_v[0], cmp_w2_v[0], w_nsa_out[0], w_out[0], norm2_g[0], w_router[0], router_bias[0],
                  w_sh_gate[0], w_sh_up[0], w_sh_down[0], w_gate[0], w_up[0], w_down[0], final_g)
```

```python
import functools

import jax
import jax.numpy as jnp
from jax import lax
from jax.experimental import pallas as pl
from jax.experimental.pallas import tpu as pltpu

BF16 = jnp.bfloat16
F32 = jnp.float32
I32 = jnp.int32

D_CONV = 512
CONV_K = 31
N_HEADS = 16
N_KV_GROUPS = 4
HEADS_PER_GROUP = 4
HEAD_DIM = 64
CMP_BLOCK = 32
CMP_STRIDE = 16
SEL_BLOCK = 64
SEL_TOPK = 16
WINDOW = 512
FORCED_SCORE = 1e4
N_EXPERTS = 256
TOP_K = 8
N_GROUPS = 8
TOPK_GROUPS = 4
ROUTED_SCALE = 2.5
EPS = 1e-6
NEG = -1e30
MASK_BIAS = -1e9
LANES = 128
HALO = 32
ATT_TILE = 256
EXPERT_BLOCK = 256
COMBINE_TILE = 256
VMEM_LIMIT = 56 * 1024 * 1024


def _params(*sem):
    return pltpu.CompilerParams(dimension_semantics=sem, vmem_limit_bytes=VMEM_LIMIT)


def _dot(a, b):
    return jnp.dot(a, b, preferred_element_type=F32)


def _dot_nt(a, b):
    return lax.dot_general(a, b, (((1,), (1,)), ((), ())), preferred_element_type=F32)


def _split_bf16(x):
    hi = x.astype(BF16)
    lo = (x - hi.astype(F32)).astype(BF16)
    return hi, lo


def _ada_kernel(c_ref, w_ref, b_ref, o_ref):
    c = c_ref[...]
    a = c * jax.nn.sigmoid(c)
    o_ref[...] = _dot(a.astype(BF16), w_ref[...].astype(BF16)) + b_ref[...]


def ada_modulation(c, w_ada, b_ada):
    B, D = c.shape
    N = w_ada.shape[1]
    rows = 8
    c8 = jnp.zeros((rows, D), F32).at[:B].set(c)
    tn = 1024
    out = pl.pallas_call(
        _ada_kernel,
        grid=(N // tn,),
        in_specs=[pl.BlockSpec((rows, D), lambda j: (0, 0)),
                  pl.BlockSpec((D, tn), lambda j: (0, j)),
                  pl.BlockSpec((1, tn), lambda j: (0, j))],
        out_specs=pl.BlockSpec((rows, tn), lambda j: (0, j)),
        out_shape=jax.ShapeDtypeStruct((rows, N), F32),
        compiler_params=_params("arbitrary"),
        name="ada",
    )(c8, w_ada, b_ada.reshape(1, N))
    return out[:B]


def _modulated_rmsnorm(x, g, sc, sh):
    y = x * lax.rsqrt(jnp.mean(x * x, axis=-1, keepdims=True) + EPS)
    return (y * g) * (1.0 + sc) + sh


def _in_proj_kernel(x_ref, g_ref, sc_ref, sh_ref, w_ref, b_ref, o_ref, h_scr):
    @pl.when(pl.program_id(1) == 0)
    def _():
        h = _modulated_rmsnorm(x_ref[...], g_ref[...], sc_ref[...], sh_ref[...])
        h_scr[...] = h.astype(BF16)

    o_ref[...] = (_dot(h_scr[...], w_ref[...]) + b_ref[...]).astype(BF16)


def in_proj(x2, norm_g, sc, sh, w, b, seq, tm=512, tn=1152):
    T, D = x2.shape
    NP = w.shape[1]
    per_b = seq // tm
    return pl.pallas_call(
        _in_proj_kernel,
        grid=(T // tm, NP // tn),
        in_specs=[pl.BlockSpec((tm, D), lambda i, j: (i, 0)),
                  pl.BlockSpec((1, D), lambda i, j: (0, 0)),
                  pl.BlockSpec((None, 1, D), lambda i, j: (i // per_b, 0, 0)),
                  pl.BlockSpec((None, 1, D), lambda i, j: (i // per_b, 0, 0)),
                  pl.BlockSpec((D, tn), lambda i, j: (0, j)),
                  pl.BlockSpec((1, tn), lambda i, j: (0, j))],
        out_specs=pl.BlockSpec((tm, tn), lambda i, j: (i, j)),
        out_shape=jax.ShapeDtypeStruct((T, NP), BF16),
        scratch_shapes=[pltpu.VMEM((tm, D), BF16)],
        compiler_params=_params("arbitrary", "arbitrary"),
        name="in_proj",
    )(x2, norm_g.reshape(1, D), sc[:, None, :], sh[:, None, :], w, b.reshape(1, NP))


def _conv_kernel(a_ref, gt_ref, pa_ref, pg_ref, mg_ref, cw_ref, cb_ref, lg_ref, lb_ref, wo_ref, bo_ref,
                 o_ref, u_scr, c_scr, *, ts, chunk):
    i = pl.program_id(1)
    a = a_ref[...].astype(F32)
    u_scr[HALO:, :] = a * jax.nn.sigmoid(gt_ref[...].astype(F32))
    pa = pa_ref[...].astype(F32)
    prev = pa * jax.nn.sigmoid(pg_ref[...].astype(F32))
    u_scr[:HALO, :] = jnp.where(i > 0, prev, 0.0)
    off = HALO - (CONV_K - 1)
    for r0 in range(0, ts, chunk):
        acc = jnp.broadcast_to(cb_ref[...], (chunk, D_CONV))
        for k in range(CONV_K):
            acc = acc + cw_ref[k:k + 1, :] * u_scr[r0 + off + k:r0 + off + k + chunk, :]
        c_scr[r0:r0 + chunk, :] = acc
    v = c_scr[...]
    mu = jnp.mean(v, axis=-1, keepdims=True)
    var = jnp.mean(jnp.square(v - mu), axis=-1, keepdims=True)
    y = (v - mu) * lax.rsqrt(var + EPS) * lg_ref[...] + lb_ref[...]
    y = y * jax.nn.sigmoid(y)
    ya = _dot(y.astype(BF16), wo_ref[...]) + bo_ref[...]
    o_ref[...] = (jax.nn.sigmoid(mg_ref[...].astype(F32)) * ya).astype(BF16)


def conformer_conv(proj, B, S, D, merge_col, conv_w, conv_b, ln_g, ln_b, w_o, b_o, ts=512, chunk=64):
    T = B * S
    nt = S // ts
    hb = ts // HALO
    prev_idx = lambda b, i: (jnp.maximum((b * nt + i) * hb - 1, 0), 0)
    prev_idx_g = lambda b, i: (jnp.maximum((b * nt + i) * hb - 1, 0), 1)
    row = lambda v: v.reshape(1, -1)
    return pl.pallas_call(
        functools.partial(_conv_kernel, ts=ts, chunk=chunk),
        grid=(B, nt),
        in_specs=[pl.BlockSpec((ts, D_CONV), lambda b, i: (b * nt + i, 0)),
                  pl.BlockSpec((ts, D_CONV), lambda b, i: (b * nt + i, 1)),
                  pl.BlockSpec((HALO, D_CONV), prev_idx),
                  pl.BlockSpec((HALO, D_CONV), prev_idx_g),
                  pl.BlockSpec((ts, D), lambda b, i: (b * nt + i, merge_col // D)),
                  pl.BlockSpec((CONV_K, D_CONV), lambda b, i: (0, 0)),
                  pl.BlockSpec((1, D_CONV), lambda b, i: (0, 0)),
                  pl.BlockSpec((1, D_CONV), lambda b, i: (0, 0)),
                  pl.BlockSpec((1, D_CONV), lambda b, i: (0, 0)),
                  pl.BlockSpec((D_CONV, D), lambda b, i: (0, 0)),
                  pl.BlockSpec((1, D), lambda b, i: (0, 0))],
        out_specs=pl.BlockSpec((ts, D), lambda b, i: (b * nt + i, 0)),
        out_shape=jax.ShapeDtypeStruct((T, D), BF16),
        scratch_shapes=[pltpu.VMEM((ts + HALO, D_CONV), F32), pltpu.VMEM((ts, D_CONV), F32)],
        compiler_params=_params("arbitrary", "arbitrary"),
        name="conv",
    )(proj, proj, proj, proj, proj, conv_w, row(conv_b), row(ln_g), row(ln_b), w_o.astype(BF16), row(b_o))


def _compress_kernel(z_ref, pe_ref, w1_ref, w2_ref, o_ref, *, nc):
    half = w1_ref.shape[0] // 2
    z = z_ref[...]
    first = _dot(z, w1_ref[:half, :])
    second = _dot(z, w1_ref[half:, :])
    bias = _dot(pe_ref[...], w1_ref[...])[0:1, :]
    hid = first + pltpu.roll(second, nc - 1, 0) + bias
    act = jax.nn.gelu(hid)
    out = _dot(act.astype(BF16), w2_ref[...])
    rows = lax.broadcasted_iota(I32, out.shape, 0)
    o_ref[...] = jnp.where(rows < nc - 1, out, 0.0).astype(BF16)


def compress_kv(zc, pe, w1, w2):
    B, _, G, NC, W = zc.shape
    H = w1.shape[-1]
    pe8 = jnp.broadcast_to(pe.reshape(2, 1, 2 * W), (2, 8, 2 * W)).astype(BF16)
    return pl.pallas_call(
        functools.partial(_compress_kernel, nc=NC),
        grid=(B, 2, G),
        in_specs=[pl.BlockSpec((None, None, None, NC, W), lambda b, s, g: (b, s, g, 0, 0)),
                  pl.BlockSpec((None, 8, 2 * W), lambda b, s, g: (s, 0, 0)),
                  pl.BlockSpec((None, 2 * W, H), lambda b, s, g: (s, 0, 0)),
                  pl.BlockSpec((None, H, HEAD_DIM), lambda b, s, g: (s, 0, 0))],
        out_specs=pl.BlockSpec((None, None, None, NC, HEAD_DIM), lambda b, s, g: (b, s, g, 0, 0)),
        out_shape=jax.ShapeDtypeStruct((B, 2, G, NC, HEAD_DIM), BF16),
        compiler_params=_params("arbitrary", "arbitrary", "arbitrary"),
        name="compress",
    )(zc, pe8, w1.astype(BF16), w2.astype(BF16))


def _cmp_attn_kernel(q_ref, kc_ref, vc_ref, ov_ref, oc_ref, sb_ref, *, tq, ns):
    i = pl.program_id(2)
    t0 = i * tq
    nc = kc_ref.shape[0]
    q = q_ref[...]
    kc = kc_ref[...]
    vc = vc_ref[...]
    t_ids = t0 + lax.broadcasted_iota(I32, (tq, nc), 0)
    n_ids = lax.broadcasted_iota(I32, (tq, nc), 1)
    valid = (CMP_STRIDE * n_ids + (CMP_BLOCK - 1)) <= t_ids
    psum = jnp.zeros((tq, nc), F32)
    for h in range(HEADS_PER_GROUP):
        qh = q[:, h * HEAD_DIM:(h + 1) * HEAD_DIM]
        s = _dot_nt(qh, kc) * (HEAD_DIM ** -0.5)
        s = jnp.where(valid, s, NEG)
        m = jnp.max(s, axis=-1, keepdims=True)
        e = jnp.where(valid, jnp.exp(s - m), 0.0)
        den = jnp.sum(e, axis=-1, keepdims=True)
        p = e / jnp.where(den > 0.0, den, 1.0)
        oc_ref[:, h * HEAD_DIM:(h + 1) * HEAD_DIM] = _dot(p.astype(BF16), vc).astype(BF16)
        psum = psum + p
    ps_hi, ps_lo = _split_bf16(psum)
    imp = _dot(ps_hi, ov_ref[...]) + _dot(ps_lo, ov_ref[...])
    lane = lax.broadcasted_iota(I32, (tq, LANES), 1)
    jq = (t0 + lax.broadcasted_iota(I32, (tq, LANES), 0)) >> (SEL_BLOCK.bit_length() - 1)
    forced = (lane == 0) | (lane == jq) | (lane == jq - 1)
    imp = jnp.where(forced, FORCED_SCORE, jnp.where(lane <= jq, imp, -1.0))
    work = jnp.where(lane < ns, imp, -jnp.inf)
    lane_f = lane.astype(F32)
    sel = jnp.zeros((tq, LANES), jnp.bool_)
    for _ in range(SEL_TOPK):
        m = jnp.max(work, axis=-1, keepdims=True)
        first = jnp.min(jnp.where(work == m, lane_f, float(LANES)), axis=-1, keepdims=True)
        pick = lane_f == first
        sel = sel | pick
        work = jnp.where(pick, -jnp.inf, work)
    sb_ref[...] = jnp.where(sel & (lane <= jq), 0.0, MASK_BIAS).astype(BF16)


def cmp_attention(proj, kvc, overlap, B, S, q_col, tq=ATT_TILE):
    G = N_KV_GROUPS
    NC = kvc.shape[3]
    nt = S // tq
    gw = HEADS_PER_GROUP * HEAD_DIM
    ns = S // SEL_BLOCK
    assert SEL_TOPK <= ns <= LANES
    return pl.pallas_call(
        functools.partial(_cmp_attn_kernel, tq=tq, ns=ns),
        grid=(B, G, nt),
        in_specs=[pl.BlockSpec((tq, gw), lambda b, g, i: (b * nt + i, q_col // gw + g)),
                  pl.BlockSpec((None, None, None, NC, HEAD_DIM), lambda b, g, i: (b, 0, g, 0, 0)),
                  pl.BlockSpec((None, None, None, NC, HEAD_DIM), lambda b, g, i: (b, 1, g, 0, 0)),
                  pl.BlockSpec((NC, LANES), lambda b, g, i: (0, 0))],
        out_specs=[pl.BlockSpec((tq, gw), lambda b, g, i: (b * nt + i, g)),
                   pl.BlockSpec((None, None, tq, LANES), lambda b, g, i: (b, g, i, 0))],
        out_shape=[jax.ShapeDtypeStruct((B * S, G * gw), BF16),
                   jax.ShapeDtypeStruct((B, G, S, LANES), BF16)],
        compiler_params=_params("arbitrary", "arbitrary", "arbitrary"),
        name="cmp_attn",
    )(proj, kvc, kvc, overlap)


def _nsa_kernel(q_ref, sb_ref, ksa_ref, vs_ref, kw_ref, vw_ref, oc_ref, ng_ref, o_ref,
                qa_scr, ms_scr, as_scr, mw_scr, aw_scr, *, tq, unroll):
    g = pl.program_id(1)
    i = pl.program_id(2)
    HG = HEADS_PER_GROUP
    R = HG * tq
    q = q_ref[...] * (HEAD_DIM ** -0.5)
    sb = sb_ref[...]
    for h in range(HG):
        qa_scr[h * tq:(h + 1) * tq, :LANES] = sb
        qa_scr[h * tq:(h + 1) * tq, LANES:] = q[:, h * HEAD_DIM:(h + 1) * HEAD_DIM]

    q_loc = lax.broadcasted_iota(I32, (R, tq), 0) & (tq - 1)
    k_loc = lax.broadcasted_iota(I32, (R, tq), 1)
    causal = lambda: k_loc <= q_loc
    strictly_upper = lambda: k_loc > q_loc

    def step(kt, mask, k_ref, v_ref, q_lo, m_scr, acc_scr):
        start = pl.multiple_of(kt * tq, tq)
        s = _dot(qa_scr[:, q_lo:], k_ref[:, pl.ds(start, tq)])
        if mask is not None:
            s = jnp.where(mask(), s, NEG)
        m_prev = m_scr[...]
        m_new = jnp.maximum(m_prev, jnp.max(s, axis=-1, keepdims=True))
        alpha = jnp.exp(m_prev - m_new)
        p = jnp.concatenate([jnp.exp(s[:, j * LANES:(j + 1) * LANES] - m_new) for j in range(tq // LANES)], axis=-1)
        acc_scr[...] = alpha * acc_scr[...] + _dot(p.astype(BF16), v_ref[pl.ds(start, tq), :])
        m_scr[...] = m_new

    sel = functools.partial(step, k_ref=ksa_ref, v_ref=vs_ref, q_lo=0, m_scr=ms_scr, acc_scr=as_scr)
    win = functools.partial(step, k_ref=kw_ref, v_ref=vw_ref, q_lo=LANES, m_scr=mw_scr, acc_scr=aw_scr)

    for m_scr, acc_scr in ((ms_scr, as_scr), (mw_scr, aw_scr)):
        m_scr[...] = jnp.full(m_scr.shape, NEG, F32)
        acc_scr[...] = jnp.zeros(acc_scr.shape, F32)

    def sel_unrolled(kp, carry):
        for u in range(unroll):
            sel(unroll * kp + u, None)
        return carry

    def sel_single(kt, carry):
        sel(kt, None)
        return carry

    lax.fori_loop(0, i // unroll, sel_unrolled, 0)
    lax.fori_loop((i // unroll) * unroll, i, sel_single, 0)

    @pl.when(i >= 2)
    def _():
        win(i - 2, strictly_upper)
        sel(i, causal)
        win(i - 1, None)
        win(i, causal)

    @pl.when(i < 2)
    def _():
        sel(i, causal)

        @pl.when(i == 1)
        def _():
            win(0, None)

        win(i, causal)

    acc_s = as_scr[...]
    o_sel = acc_s[:, :HEAD_DIM] / acc_s[:, HEAD_DIM:HEAD_DIM + 1]
    acc_w = aw_scr[...]
    o_win = acc_w[:, :HEAD_DIM] / acc_w[:, HEAD_DIM:HEAD_DIM + 1]

    gates = jax.nn.sigmoid(ng_ref[...].astype(F32))
    oc = oc_ref[...].astype(F32)
    lane = lax.broadcasted_iota(I32, gates.shape, 1)
    for h in range(HG):
        c0 = (g * HG + h) * 3
        gc = jnp.sum(jnp.where(lane == c0, gates, 0.0), axis=-1, keepdims=True)
        gs = jnp.sum(jnp.where(lane == c0 + 1, gates, 0.0), axis=-1, keepdims=True)
        gw = jnp.sum(jnp.where(lane == c0 + 2, gates, 0.0), axis=-1, keepdims=True)
        rows = slice(h * tq, (h + 1) * tq)
        cols = slice(h * HEAD_DIM, (h + 1) * HEAD_DIM)
        o_ref[:, cols] = (gc * oc[:, cols] + gs * o_sel[rows, :] + gw * o_win[rows, :]).astype(BF16)


def nsa_attention(proj, selbias, ksa_t, vs_aug, kw_t, vw_aug, oc, B, S, q_col, ng_col, tq=ATT_TILE, unroll=4):
    G = N_KV_GROUPS
    nt = S // tq
    gw = HEADS_PER_GROUP * HEAD_DIM
    assert WINDOW == 2 * tq and tq & (tq - 1) == 0
    R = HEADS_PER_GROUP * tq
    per_group = lambda rows, cols: pl.BlockSpec((None, None, rows, cols), lambda b, g, i: (b, g, 0, 0))
    return pl.pallas_call(
        functools.partial(_nsa_kernel, tq=tq, unroll=unroll),
        grid=(B, G, nt),
        in_specs=[pl.BlockSpec((tq, gw), lambda b, g, i: (b * nt + i, q_col // gw + g)),
                  pl.BlockSpec((None, None, tq, LANES), lambda b, g, i: (b, g, i, 0)),
                  per_group(LANES + HEAD_DIM, S), per_group(S, LANES), per_group(HEAD_DIM, S), per_group(S, LANES),
                  pl.BlockSpec((tq, gw), lambda b, g, i: (b * nt + i, g)),
                  pl.BlockSpec((tq, LANES), lambda b, g, i: (b * nt + i, ng_col // LANES))],
        out_specs=pl.BlockSpec((tq, gw), lambda b, g, i: (b * nt + i, g)),
        out_shape=jax.ShapeDtypeStruct((B * S, G * gw), BF16),
        scratch_shapes=[pltpu.VMEM((R, LANES + HEAD_DIM), BF16),
                        pltpu.VMEM((R, LANES), F32), pltpu.VMEM((R, LANES), F32),
                        pltpu.VMEM((R, LANES), F32), pltpu.VMEM((R, LANES), F32)],
        compiler_params=_params("arbitrary", "arbitrary", "arbitrary"),
        name="nsa_attn",
    )(proj, selbias, ksa_t, vs_aug, kw_t, vw_aug, oc, proj)


def _route(logits, rbias):
    tm, E = logits.shape
    gsz = E // N_GROUPS
    scores = jax.nn.sigmoid(logits)
    sfc = scores + rbias
    lane = lax.broadcasted_iota(I32, (tm, E), 1)
    lane_f = lane.astype(F32)
    grp = lane >> (gsz.bit_length() - 1)
    gscore = []
    for g in range(N_GROUPS):
        v = jnp.where(grp == g, sfc, -jnp.inf)
        m1 = jnp.max(v, axis=-1, keepdims=True)
        cnt = jnp.sum(jnp.where(v == m1, 1.0, 0.0), axis=-1, keepdims=True)
        m2 = jnp.max(jnp.where(v < m1, v, -jnp.inf), axis=-1, keepdims=True)
        gscore.append(m1 + jnp.where(cnt >= 2.0, m1, m2))
    emask = jnp.zeros((tm, E), jnp.bool_)
    for g in range(N_GROUPS):
        rank = jnp.zeros((tm, 1), F32)
        for o in range(N_GROUPS):
            if o == g:
                continue
            ahead = (gscore[o] > gscore[g]) | ((gscore[o] == gscore[g]) & (o < g))
            rank = rank + jnp.where(ahead, 1.0, 0.0)
        emask = emask | ((grp == g) & (rank < float(TOPK_GROUPS)))
    work = jnp.where(emask, sfc, NEG)
    out_lane = lax.broadcasted_iota(I32, (tm, LANES), 1)
    idx_out = jnp.zeros((tm, LANES), F32)
    w_out = jnp.zeros((tm, LANES), F32)
    wsum = jnp.zeros((tm, 1), F32)
    for r in range(TOP_K):
        m = jnp.max(work, axis=-1, keepdims=True)
        first = jnp.min(jnp.where(work == m, lane_f, float(E)), axis=-1, keepdims=True)
        pick = lane_f == first
        w = jnp.sum(jnp.where(pick, scores, 0.0), axis=-1, keepdims=True)
        idx_out = jnp.where(out_lane == r, first, idx_out)
        w_out = jnp.where(out_lane == r, w, w_out)
        wsum = wsum + w
        work = jnp.where(pick, -jnp.inf, work)
    return idx_out.astype(I32), w_out / wsum * ROUTED_SCALE


def _out_proj_kernel(o_ref, za_ref, mb_ref, x_ref, g1_ref, n2_ref, sc_ref, sh_ref, wn_ref, wo_ref,
                     wrh_ref, wrl_ref, rb_ref, x1_ref, h2_ref, idx_ref, wt_ref):
    yb = _dot(o_ref[...], wn_ref[...])
    z = za_ref[...].astype(F32) + jax.nn.sigmoid(mb_ref[...].astype(F32)) * yb
    mix = _dot(z.astype(BF16), wo_ref[...])
    x1 = x_ref[...] + g1_ref[...] * mix
    x1_ref[...] = x1
    h2 = _modulated_rmsnorm(x1, n2_ref[...], sc_ref[...], sh_ref[...])
    h2_ref[...] = h2
    hi, lo = _split_bf16(h2)
    logits = _dot(hi, wrh_ref[...]) + (_dot(lo, wrh_ref[...]) + _dot(hi, wrl_ref[...]))
    idx, wts = _route(logits, rb_ref[...])
    idx_ref[...] = idx
    wt_ref[...] = wts


def out_proj(o, za, proj, x2, g1, norm_g, sc, sh, w_nsa_out, w_out, w_router, router_bias, seq, mb_col, tm=256):
    T, D = x2.shape
    E = w_router.shape[1]
    per_b = seq // tm
    wr_hi, wr_lo = _split_bf16(w_router)
    tok = lambda i: (i, 0)
    const = lambda i: (0, 0)
    per_batch = pl.BlockSpec((None, 1, D), lambda i: (i // per_b, 0, 0))
    return pl.pallas_call(
        _out_proj_kernel,
        grid=(T // tm,),
        in_specs=[pl.BlockSpec((tm, D), tok), pl.BlockSpec((tm, D), tok),
                  pl.BlockSpec((tm, D), lambda i: (i, mb_col // D)),
                  pl.BlockSpec((tm, D), tok), per_batch,
                  pl.BlockSpec((1, D), const), per_batch, per_batch,
                  pl.BlockSpec((D, D), const), pl.BlockSpec((D, D), const),
                  pl.BlockSpec((D, E), const), pl.BlockSpec((D, E), const), pl.BlockSpec((1, E), const)],
        out_specs=[pl.BlockSpec((tm, D), tok), pl.BlockSpec((tm, D), tok),
                   pl.BlockSpec((tm, LANES), tok), pl.BlockSpec((tm, LANES), tok)],
        out_shape=[jax.ShapeDtypeStruct((T, D), F32), jax.ShapeDtypeStruct((T, D), F32),
                   jax.ShapeDtypeStruct((T, LANES), I32), jax.ShapeDtypeStruct((T, LANES), F32)],
        compiler_params=_params("arbitrary"),
        name="out_proj",
    )(o, za, proj, x2, g1[:, None, :], norm_g.reshape(1, D), sc[:, None, :], sh[:, None, :],
      w_nsa_out.astype(BF16), w_out.astype(BF16), wr_hi, wr_lo, router_bias.reshape(1, E))


def _experts_kernel(blk_ref, ex_ref, lo_ref, hi_ref, first_ref, last_ref, ni_ref,
                    ord_hbm, h_hbm, wg_ref, wu_ref, wd_ref, y_hbm,
                    ord_smem, x_buf, y_buf, ord_sem, row_sem, out_sem, *, bm, nblk):
    w = pl.program_id(0)
    n_items = ni_ref[0]

    def ord_copy(b, s):
        return pltpu.make_async_copy(ord_hbm.at[b], ord_smem.at[s], ord_sem.at[s])

    def gather_row(s, r, a):
        return pltpu.make_async_copy(h_hbm.at[pl.ds(a >> (TOP_K.bit_length() - 1), 1), :],
                                     x_buf.at[s, pl.ds(r, 1), :], row_sem.at[s])

    def scatter_row(s, r, a):
        return pltpu.make_async_copy(y_buf.at[s, pl.ds(r, 1), :], y_hbm.at[pl.ds(a, 1), :], out_sem.at[s])

    def issue(make_copy, s):
        def body(r, carry):
            make_copy(s, r, ord_smem[s, r]).start()
            return carry
        lax.fori_loop(0, bm, body, 0, unroll=8)

    def wait_gather(s):
        pltpu.make_async_copy(h_hbm.at[pl.ds(0, bm), :], x_buf.at[s], row_sem.at[s]).wait()

    def wait_scatter(s):
        pltpu.make_async_copy(y_buf.at[s], y_hbm.at[pl.ds(0, bm), :], out_sem.at[s]).wait()

    @pl.when(w < n_items)
    def _():
        b = blk_ref[w]
        s = b % 2
        is_first = first_ref[w] == 1
        is_last = last_ref[w] == 1

        @pl.when(w == 0)
        def _():
            ord_copy(0, 0).start()
            ord_copy(0, 0).wait()
            issue(gather_row, 0)
            if nblk > 1:
                ord_copy(1, 1).start()

        @pl.when(is_first)
        def _():
            @pl.when(b + 1 < nblk)
            def _():
                ord_copy(b + 1, 1 - s).wait()
                issue(gather_row, 1 - s)

            wait_gather(s)

            @pl.when(b >= 2)
            def _():
                wait_scatter(s)

        x = x_buf[s].astype(BF16)
        gate = _dot(x, wg_ref[...])
        up = _dot(x, wu_ref[...])
        hid = gate * jax.nn.sigmoid(gate) * up
        y = _dot(hid.astype(BF16), wd_ref[...])
        rows = lax.broadcasted_iota(I32, (bm, 1), 0)
        mine = (rows >= lo_ref[w]) & (rows < hi_ref[w])

        @pl.when(is_first)
        def _():
            y_buf[s] = jnp.where(mine, y, 0.0)

        @pl.when(jnp.logical_not(is_first))
        def _():
            y_buf[s] = jnp.where(mine, y, y_buf[s])

        @pl.when(is_last)
        def _():
            issue(scatter_row, s)

            @pl.when(b + 2 < nblk)
            def _():
                ord_copy(b + 2, s).start()

        @pl.when(w == n_items - 1)
        def _():
            wait_scatter(s)
            if nblk > 1:
                wait_scatter(1 - s)


def routed_experts(h2, order_blocks, plan, w_gate, w_up, w_down, bm=EXPERT_BLOCK):
    T, D = h2.shape
    nblk = order_blocks.shape[0]
    E, _, DE = w_gate.shape
    n_work = plan[0].shape[0]
    by_expert = lambda w, blk, ex, *_: (ex[w], 0, 0)
    grid_spec = pltpu.PrefetchScalarGridSpec(
        num_scalar_prefetch=len(plan),
        grid=(n_work,),
        in_specs=[pl.BlockSpec(memory_space=pl.ANY),
                  pl.BlockSpec(memory_space=pl.ANY),
                  pl.BlockSpec((None, D, DE), by_expert),
                  pl.BlockSpec((None, D, DE), by_expert),
                  pl.BlockSpec((None, DE, D), by_expert)],
        out_specs=pl.BlockSpec(memory_space=pl.ANY),
        scratch_shapes=[pltpu.SMEM((2, bm), I32), pltpu.VMEM((2, bm, D), F32), pltpu.VMEM((2, bm, D), F32),
                        pltpu.SemaphoreType.DMA((2,)), pltpu.SemaphoreType.DMA((2,)), pltpu.SemaphoreType.DMA((2,))],
    )
    return pl.pallas_call(
        functools.partial(_experts_kernel, bm=bm, nblk=nblk),
        grid_spec=grid_spec,
        out_shape=jax.ShapeDtypeStruct((nblk * bm, D), F32),
        compiler_params=_params("arbitrary"),
        name="experts",
    )(*plan, order_blocks, h2, w_gate, w_up, w_down)


def _combine_kernel(y8_ref, h2_ref, x1_ref, wt_ref, g2_ref, fg_ref, wsg_ref, wsu_ref, wsd_ref, o_ref):
    D = o_ref.shape[1]
    h2 = h2_ref[...].astype(BF16)
    gate = _dot(h2, wsg_ref[...])
    up = _dot(h2, wsu_ref[...])
    y = _dot((gate * jax.nn.sigmoid(gate) * up).astype(BF16), wsd_ref[...])
    wt = wt_ref[...]
    for k in range(TOP_K):
        y = y + wt[:, k:k + 1] * y8_ref[:, k * D:(k + 1) * D]
    x2 = x1_ref[...] + g2_ref[...] * y
    o_ref[...] = x2 * lax.rsqrt(jnp.mean(x2 * x2, axis=-1, keepdims=True) + EPS) * fg_ref[...]


def combine(y8, h2, x1, wts, g2, final_g, w_sh_gate, w_sh_up, w_sh_down, seq, tj=COMBINE_TILE):
    T, D = x1.shape
    DS = w_sh_gate.shape[1]
    per_b = seq // tj
    tok = lambda i: (i, 0)
    const = lambda i: (0, 0)
    return pl.pallas_call(
        _combine_kernel,
        grid=(T // tj,),
        in_specs=[pl.BlockSpec((tj, TOP_K * D), tok),
                  pl.BlockSpec((tj, D), tok), pl.BlockSpec((tj, D), tok), pl.BlockSpec((tj, LANES), tok),
                  pl.BlockSpec((None, 1, D), lambda i: (i // per_b, 0, 0)),
                  pl.BlockSpec((1, D), const),
                  pl.BlockSpec((D, DS), const), pl.BlockSpec((D, DS), const), pl.BlockSpec((DS, D), const)],
        out_specs=pl.BlockSpec((tj, D), tok),
        out_shape=jax.ShapeDtypeStruct((T, D), F32),
        compiler_params=_params("arbitrary"),
        name="combine",
    )(y8, h2, x1, wts, g2[:, None, :], final_g.reshape(1, D),
      w_sh_gate.astype(BF16), w_sh_up.astype(BF16), w_sh_down.astype(BF16))


def _dispatch_plan(idx, T, bm):
    A = T * TOP_K
    assert A % bm == 0
    nblk = A // bm
    E = N_EXPERTS
    sorted_e, order = lax.sort((idx.reshape(A), jnp.arange(A, dtype=I32)), num_keys=1)
    bounds = jnp.searchsorted(sorted_e, jnp.arange(E + 1, dtype=I32), side='left').astype(I32)
    start, end = bounds[:-1], bounds[1:]
    first_blk = start // bm
    n_it = jnp.where(end > start, (end - 1) // bm - first_blk + 1, 0)
    it_end = jnp.cumsum(n_it)
    it_start = it_end - n_it
    n_items = it_end[-1]
    n_work = nblk + E - 1
    w = jnp.arange(n_work, dtype=I32)
    wc = jnp.minimum(w, n_items - 1)
    ex = jnp.searchsorted(it_end, wc, side='right').astype(I32)
    blk = first_blk[ex] + wc - it_start[ex]
    lo = jnp.maximum(start[ex], blk * bm) - blk * bm
    hi = jnp.minimum(end[ex], (blk + 1) * bm) - blk * bm
    first = jnp.concatenate([jnp.ones((1,), I32), (blk[1:] != blk[:-1]).astype(I32)])
    last = jnp.concatenate([(blk[1:] != blk[:-1]).astype(I32), jnp.ones((1,), I32)])
    last = jnp.where(w == n_items - 1, 1, last)
    plan = (blk.astype(I32), ex, lo.astype(I32), hi.astype(I32), first, last, n_items.astype(I32).reshape(1))
    return order.reshape(nblk, bm), plan


def _layer(x, c, w_ada, b_ada, norm1_g, w_in, b_in, conv_w, conv_b, conv_ln_g, conv_ln_b, w_conv_out, b_conv_out,
           cmp_pe_k, cmp_w1_k, cmp_w2_k, cmp_pe_v, cmp_w1_v, cmp_w2_v, w_nsa_out, w_out, norm2_g, w_router,
           router_bias, w_sh_gate, w_sh_up, w_sh_down, w_gate, w_up, w_down, final_g):
    B, S, D = x.shape
    T = B * S
    G, dh = N_KV_GROUPS, HEAD_DIM
    n_q = N_HEADS * dh
    n_kv = 3 * 2 * G * dh
    n_ng = 3 * N_HEADS
    mod = ada_modulation(c, w_ada, b_ada)
    sh1, sc1, g1, sh2, sc2, g2 = jnp.split(mod, 6, axis=-1)

    c_q = 2 * D_CONV
    c_kv = c_q + n_q
    c_ng = c_kv + n_kv
    c_mg = c_ng + n_ng
    pad = LANES - n_ng
    w_r = jnp.concatenate([w_in[:, :c_kv], w_in[:, c_mg:], w_in[:, c_kv:c_mg], jnp.zeros((D, pad), F32)], axis=1)
    b_r = jnp.concatenate([b_in[:c_kv], b_in[c_mg:], b_in[c_kv:c_mg], jnp.zeros((pad,), F32)])
    col_merge = c_kv
    col_kv = col_merge + 2 * D
    col_ng = col_kv + n_kv
    x2 = x.reshape(T, D)
    proj = in_proj(x2, norm1_g, sc1, sh1, w_r.astype(BF16), b_r, S)

    za = conformer_conv(proj, B, S, D, col_merge, conv_w, conv_b, conv_ln_g, conv_ln_b, w_conv_out, b_conv_out)

    kv = proj[:, col_kv:col_kv + n_kv].reshape(B, S, 3, 2, G, dh)
    NC = S // CMP_STRIDE
    zc = kv[:, :, 0].reshape(B, NC, CMP_STRIDE, 2, G, dh).transpose(0, 3, 4, 1, 2, 5).reshape(B, 2, G, NC, CMP_STRIDE * dh)
    kvc = compress_kv(zc, jnp.stack([cmp_pe_k, cmp_pe_v]), jnp.stack([cmp_w1_k, cmp_w1_v]),
                      jnp.stack([cmp_w2_k, cmp_w2_v]))

    cs = CMP_STRIDE * jnp.arange(NC)[:, None]
    ss = SEL_BLOCK * jnp.arange(LANES)[None, :]
    overlap = ((cs <= ss + SEL_BLOCK - 1) & (cs + CMP_BLOCK - 1 >= ss) & (jnp.arange(NC)[:, None] < NC - 1)
               & (jnp.arange(LANES)[None, :] < S // SEL_BLOCK)).astype(BF16)
    oc, selbias = cmp_attention(proj, kvc, overlap, B, S, c_q)

    keys_t = lambda br: kv[:, :, br, 0].transpose(0, 2, 3, 1)
    ones_col = jnp.concatenate([jnp.ones((B, G, S, 1), BF16), jnp.zeros((B, G, S, LANES - dh - 1), BF16)], axis=-1)
    vals_aug = lambda br: jnp.concatenate([kv[:, :, br, 1].transpose(0, 2, 1, 3), ones_col], axis=-1)
    onehot_t = (jnp.arange(LANES)[:, None] == jnp.arange(S)[None, :] // SEL_BLOCK).astype(BF16)
    ksa_t = jnp.concatenate([jnp.broadcast_to(onehot_t, (B, G, LANES, S)), keys_t(1)], axis=2)
    o = nsa_attention(proj, selbias, ksa_t, vals_aug(1), keys_t(2), vals_aug(2), oc, B, S, c_q, col_ng)

    x1, h2, idx, wts = out_proj(o, za, proj, x2, g1, norm2_g, sc2, sh2, w_nsa_out, w_out, w_router, router_bias,
                                S, col_merge + D)

    order_blocks, plan = _dispatch_plan(idx[:, :TOP_K], T, EXPERT_BLOCK)
    y8 = routed_experts(h2, order_blocks, plan, w_gate.astype(BF16), w_up.astype(BF16), w_down.astype(BF16))
    out = combine(y8.reshape(T, TOP_K * D), h2, x1, wts, g2, final_g, w_sh_gate, w_sh_up, w_sh_down, S)
    return out.reshape(B, S, D)


def kernel(x, c, w_ada, b_ada, norm1_g, w_in, b_in, conv_w, conv_b, conv_ln_g, conv_ln_b, w_conv_out, b_conv_out, cmp_pe_k, cmp_w1_k, cmp_w2_k, cmp_pe_v, cmp_w1_v, cmp_w2_v, w_nsa_out, w_out, norm2_g, w_router, router_bias, w_sh_gate, w_sh_up, w_sh_down, w_gate, w_up, w_down, final_g):
    assert w_ada.shape[0] == 1, "single-layer block"
    return _layer(x, c, w_ada[0], b_ada[0], norm1_g[0], w_in[0], b_in[0], conv_w[0], conv_b[0], conv_ln_g[0],
                  conv_ln_b[0], w_conv_out[0], b_conv_out[0], cmp_pe_k[0], cmp_w1_k[0], cmp_w2_k[0], cmp_pe_v[0],
                  cmp_w1_v[0], cmp_w2_v[0], w_nsa_out[0], w_out[0], norm2_g[0], w_router[0], router_bias[0],
                  w_sh_gate[0], w_sh_up[0], w_sh_down[0], w_gate[0], w_up[0], w_down[0], final_g)
```

```python
import functools

import jax
import jax.numpy as jnp
from jax import lax
from jax.experimental import pallas as pl
from jax.experimental.pallas import tpu as pltpu

BF16 = jnp.bfloat16
F32 = jnp.float32
I32 = jnp.int32

D_CONV = 512
CONV_K = 31
N_HEADS = 16
N_KV_GROUPS = 4
HEADS_PER_GROUP = 4
HEAD_DIM = 64
CMP_BLOCK = 32
CMP_STRIDE = 16
SEL_BLOCK = 64
SEL_TOPK = 16
WINDOW = 512
FORCED_SCORE = 1e4
N_EXPERTS = 256
TOP_K = 8
N_GROUPS = 8
TOPK_GROUPS = 4
ROUTED_SCALE = 2.5
EPS = 1e-6
NEG = -1e30
MASK_BIAS = -1e9
LANES = 128
HALO = 32
ATT_TILE = 256
EXPERT_BLOCK = 256
COMBINE_TILE = 256
VMEM_LIMIT = 56 * 1024 * 1024


def _params(*sem):
    return pltpu.CompilerParams(dimension_semantics=sem, vmem_limit_bytes=VMEM_LIMIT)


def _dot(a, b):
    return jnp.dot(a, b, preferred_element_type=F32)


def _split_bf16(x):
    hi = x.astype(BF16)
    lo = (x - hi.astype(F32)).astype(BF16)
    return hi, lo


def _ada_kernel(c_ref, w_ref, b_ref, o_ref):
    c = c_ref[...]
    a = c * jax.nn.sigmoid(c)
    o_ref[...] = _dot(a.astype(BF16), w_ref[...].astype(BF16)) + b_ref[...]


def ada_modulation(c, w_ada, b_ada):
    B, D = c.shape
    N = w_ada.shape[1]
    rows = 8
    c8 = jnp.zeros((rows, D), F32).at[:B].set(c)
    tn = 1024
    out = pl.pallas_call(
        _ada_kernel,
        grid=(N // tn,),
        in_specs=[pl.BlockSpec((rows, D), lambda j: (0, 0)),
                  pl.BlockSpec((D, tn), lambda j: (0, j)),
                  pl.BlockSpec((1, tn), lambda j: (0, j))],
        out_specs=pl.BlockSpec((rows, tn), lambda j: (0, j)),
        out_shape=jax.ShapeDtypeStruct((rows, N), F32),
        compiler_params=_params("arbitrary"),
        name="ada",
    )(c8, w_ada, b_ada.reshape(1, N))
    return out[:B]


def _modulated_rmsnorm(x, g, sc, sh):
    y = x * lax.rsqrt(jnp.mean(x * x, axis=-1, keepdims=True) + EPS)
    return (y * g) * (1.0 + sc) + sh


def _in_proj_kernel(x_ref, g_ref, sc_ref, sh_ref, w_ref, b_ref, o_ref, h_scr):
    @pl.when(pl.program_id(1) == 0)
    def _():
        h = _modulated_rmsnorm(x_ref[...], g_ref[...], sc_ref[...], sh_ref[...])
        h_scr[...] = h.astype(BF16)

    o_ref[...] = (_dot(h_scr[...], w_ref[...]) + b_ref[...]).astype(BF16)


def in_proj(x2, norm_g, sc, sh, w, b, seq, tm=512, tn=1152):
    T, D = x2.shape
    NP = w.shape[1]
    per_b = seq // tm
    return pl.pallas_call(
        _in_proj_kernel,
        grid=(T // tm, NP // tn),
        in_specs=[pl.BlockSpec((tm, D), lambda i, j: (i, 0)),
                  pl.BlockSpec((1, D), lambda i, j: (0, 0)),
                  pl.BlockSpec((None, 1, D), lambda i, j: (i // per_b, 0, 0)),
                  pl.BlockSpec((None, 1, D), lambda i, j: (i // per_b, 0, 0)),
                  pl.BlockSpec((D, tn), lambda i, j: (0, j)),
                  pl.BlockSpec((1, tn), lambda i, j: (0, j))],
        out_specs=pl.BlockSpec((tm, tn), lambda i, j: (i, j)),
        out_shape=jax.ShapeDtypeStruct((T, NP), BF16),
        scratch_shapes=[pltpu.VMEM((tm, D), BF16)],
        compiler_params=_params("arbitrary", "arbitrary"),
        name="in_proj",
    )(x2, norm_g.reshape(1, D), sc[:, None, :], sh[:, None, :], w, b.reshape(1, NP))


def _conv_kernel(a_ref, gt_ref, pa_ref, pg_ref, mg_ref, cw_ref, cb_ref, lg_ref, lb_ref, wo_ref, bo_ref,
                 o_ref, u_scr, c_scr, *, ts, chunk):
    i = pl.program_id(1)
    a = a_ref[...].astype(F32)
    u_scr[HALO:, :] = a * jax.nn.sigmoid(gt_ref[...].astype(F32))
    pa = pa_ref[...].astype(F32)
    prev = pa * jax.nn.sigmoid(pg_ref[...].astype(F32))
    u_scr[:HALO, :] = jnp.where(i > 0, prev, 0.0)
    off = HALO - (CONV_K - 1)
    for r0 in range(0, ts, chunk):
        acc = jnp.broadcast_to(cb_ref[...], (chunk, D_CONV))
        for k in range(CONV_K):
            acc = acc + cw_ref[k:k + 1, :] * u_scr[r0 + off + k:r0 + off + k + chunk, :]
        c_scr[r0:r0 + chunk, :] = acc
    v = c_scr[...]
    mu = jnp.mean(v, axis=-1, keepdims=True)
    var = jnp.mean(jnp.square(v - mu), axis=-1, keepdims=True)
    y = (v - mu) * lax.rsqrt(var + EPS) * lg_ref[...] + lb_ref[...]
    y = y * jax.nn.sigmoid(y)
    ya = _dot(y.astype(BF16), wo_ref[...]) + bo_ref[...]
    o_ref[...] = (jax.nn.sigmoid(mg_ref[...].astype(F32)) * ya).astype(BF16)


def conformer_conv(proj, B, S, D, merge_col, conv_w, conv_b, ln_g, ln_b, w_o, b_o, ts=512, chunk=64):
    T = B * S
    nt = S // ts
    hb = ts // HALO
    prev_idx = lambda b, i: (jnp.maximum((b * nt + i) * hb - 1, 0), 0)
    prev_idx_g = lambda b, i: (jnp.maximum((b * nt + i) * hb - 1, 0), 1)
    row = lambda v: v.reshape(1, -1)
    return pl.pallas_call(
        functools.partial(_conv_kernel, ts=ts, chunk=chunk),
        grid=(B, nt),
        in_specs=[pl.BlockSpec((ts, D_CONV), lambda b, i: (b * nt + i, 0)),
                  pl.BlockSpec((ts, D_CONV), lambda b, i: (b * nt + i, 1)),
                  pl.BlockSpec((HALO, D_CONV), prev_idx),
                  pl.BlockSpec((HALO, D_CONV), prev_idx_g),
                  pl.BlockSpec((ts, D), lambda b, i: (b * nt + i, merge_col // D)),
                  pl.BlockSpec((CONV_K, D_CONV), lambda b, i: (0, 0)),
                  pl.BlockSpec((1, D_CONV), lambda b, i: (0, 0)),
                  pl.BlockSpec((1, D_CONV), lambda b, i: (0, 0)),
                  pl.BlockSpec((1, D_CONV), lambda b, i: (0, 0)),
                  pl.BlockSpec((D_CONV, D), lambda b, i: (0, 0)),
                  pl.BlockSpec((1, D), lambda b, i: (0, 0))],
        out_specs=pl.BlockSpec((ts, D), lambda b, i: (b * nt + i, 0)),
        out_shape=jax.ShapeDtypeStruct((T, D), BF16),
        scratch_shapes=[pltpu.VMEM((ts + HALO, D_CONV), F32), pltpu.VMEM((ts, D_CONV), F32)],
        compiler_params=_params("arbitrary", "arbitrary"),
        name="conv",
    )(proj, proj, proj, proj, proj, conv_w, row(conv_b), row(ln_g), row(ln_b), w_o.astype(BF16), row(b_o))


def _compress_kernel(z_ref, pe_ref, w1_ref, w2_ref, o_ref, *, nc):
    half = w1_ref.shape[0] // 2
    z = z_ref[...]
    first = _dot(z, w1_ref[:half, :])
    second = _dot(z, w1_ref[half:, :])
    bias = _dot(pe_ref[...], w1_ref[...])[0:1, :]
    hid = first + pltpu.roll(second, nc - 1, 0) + bias
    act = jax.nn.gelu(hid)
    out = _dot(act.astype(BF16), w2_ref[...])
    rows = lax.broadcasted_iota(I32, out.shape, 0)
    o_ref[...] = jnp.where(rows < nc - 1, out, 0.0).astype(BF16)


def compress_kv(zc, pe, w1, w2):
    B, _, G, NC, W = zc.shape
    H = w1.shape[-1]
    pe8 = jnp.broadcast_to(pe.reshape(2, 1, 2 * W), (2, 8, 2 * W)).astype(BF16)
    return pl.pallas_call(
        functools.partial(_compress_kernel, nc=NC),
        grid=(B, 2, G),
        in_specs=[pl.BlockSpec((None, None, None, NC, W), lambda b, s, g: (b, s, g, 0, 0)),
                  pl.BlockSpec((None, 8, 2 * W), lambda b, s, g: (s, 0, 0)),
                  pl.BlockSpec((None, 2 * W, H), lambda b, s, g: (s, 0, 0)),
                  pl.BlockSpec((None, H, HEAD_DIM), lambda b, s, g: (s, 0, 0))],
        out_specs=pl.BlockSpec((None, None, None, NC, HEAD_DIM), lambda b, s, g: (b, s, g, 0, 0)),
        out_shape=jax.ShapeDtypeStruct((B, 2, G, NC, HEAD_DIM), BF16),
        compiler_params=_params("arbitrary", "arbitrary", "arbitrary"),
        name="compress",
    )(zc, pe8, w1.astype(BF16), w2.astype(BF16))


def _cmp_attn_kernel(q_ref, kct_ref, vc_ref, ov_ref, oc_ref, sb_ref, qa_scr, *, tq, ns):
    i = pl.program_id(2)
    t0 = i * tq
    HG = HEADS_PER_GROUP
    R = HG * tq
    nc = kct_ref.shape[1]
    q = q_ref[...] * (HEAD_DIM ** -0.5)
    for h in range(HG):
        qa_scr[h * tq:(h + 1) * tq, :] = q[:, h * HEAD_DIM:(h + 1) * HEAD_DIM]
    t_ids = t0 + (lax.broadcasted_iota(I32, (R, nc), 0) & (tq - 1))
    n_ids = lax.broadcasted_iota(I32, (R, nc), 1)
    valid = (CMP_STRIDE * n_ids + (CMP_BLOCK - 1)) <= t_ids
    s = jnp.where(valid, _dot(qa_scr[...], kct_ref[...]), NEG)
    m = jnp.max(s, axis=-1, keepdims=True)
    e = jnp.where(valid, jnp.exp(s - m), 0.0)
    den = jnp.sum(e, axis=-1, keepdims=True)
    p = e * jnp.where(den > 0.0, 1.0 / den, 0.0)
    o = _dot(p.astype(BF16), vc_ref[...])
    for h in range(HG):
        oc_ref[:, h * HEAD_DIM:(h + 1) * HEAD_DIM] = o[h * tq:(h + 1) * tq, :].astype(BF16)
    psum = p[:tq]
    for h in range(1, HG):
        psum = psum + p[h * tq:(h + 1) * tq]
    ps_hi, ps_lo = _split_bf16(psum)
    imp = _dot(ps_hi, ov_ref[...]) + _dot(ps_lo, ov_ref[...])
    lane = lax.broadcasted_iota(I32, (tq, LANES), 1)
    jq = (t0 + lax.broadcasted_iota(I32, (tq, LANES), 0)) >> (SEL_BLOCK.bit_length() - 1)
    forced = (lane == 0) | (lane == jq) | (lane == jq - 1)
    imp = jnp.where(forced, FORCED_SCORE, jnp.where(lane <= jq, imp, -1.0))
    work = jnp.where(lane < ns, imp, -jnp.inf).T
    blk_f = lax.broadcasted_iota(I32, (LANES, tq), 0).astype(F32)
    sel = jnp.zeros((LANES, tq), F32)
    for _ in range(SEL_TOPK):
        top = jnp.max(work, axis=0, keepdims=True)
        first = jnp.min(jnp.where(work == top, blk_f, float(LANES)), axis=0, keepdims=True)
        pick = blk_f == first
        sel = jnp.where(pick, 1.0, sel)
        work = jnp.where(pick, -jnp.inf, work)
    sb_ref[...] = jnp.where((sel.T > 0.5) & (lane <= jq), 0.0, MASK_BIAS).astype(BF16)


def cmp_attention(proj, kc_t, vc, overlap, B, S, q_col, tq=ATT_TILE):
    G = N_KV_GROUPS
    NC = vc.shape[2]
    nt = S // tq
    gw = HEADS_PER_GROUP * HEAD_DIM
    ns = S // SEL_BLOCK
    assert SEL_TOPK <= ns <= LANES and tq & (tq - 1) == 0
    return pl.pallas_call(
        functools.partial(_cmp_attn_kernel, tq=tq, ns=ns),
        grid=(B, G, nt),
        in_specs=[pl.BlockSpec((tq, gw), lambda b, g, i: (b * nt + i, q_col // gw + g)),
                  pl.BlockSpec((None, None, HEAD_DIM, NC), lambda b, g, i: (b, g, 0, 0)),
                  pl.BlockSpec((None, None, NC, HEAD_DIM), lambda b, g, i: (b, g, 0, 0)),
                  pl.BlockSpec((NC, LANES), lambda b, g, i: (0, 0))],
        out_specs=[pl.BlockSpec((tq, gw), lambda b, g, i: (b * nt + i, g)),
                   pl.BlockSpec((None, None, tq, LANES), lambda b, g, i: (b, g, i, 0))],
        out_shape=[jax.ShapeDtypeStruct((B * S, G * gw), BF16),
                   jax.ShapeDtypeStruct((B, G, S, LANES), BF16)],
        scratch_shapes=[pltpu.VMEM((HEADS_PER_GROUP * tq, HEAD_DIM), BF16)],
        compiler_params=_params("arbitrary", "arbitrary", "arbitrary"),
        name="cmp_attn",
    )(proj, kc_t, vc, overlap)


def _nsa_kernel(q_ref, sb_ref, ksa_ref, vs_ref, kw_ref, vw_ref, oc_ref, ng_ref, o_ref,
                qa_scr, ms_scr, as_scr, mw_scr, aw_scr, *, tq, unroll):
    g = pl.program_id(1)
    i = pl.program_id(2)
    HG = HEADS_PER_GROUP
    R = HG * tq
    q = q_ref[...] * (HEAD_DIM ** -0.5)
    sb = sb_ref[...]
    for h in range(HG):
        qa_scr[h * tq:(h + 1) * tq, :LANES] = sb
        qa_scr[h * tq:(h + 1) * tq, LANES:] = q[:, h * HEAD_DIM:(h + 1) * HEAD_DIM]

    q_loc = lax.broadcasted_iota(I32, (R, tq), 0) & (tq - 1)
    k_loc = lax.broadcasted_iota(I32, (R, tq), 1)
    causal = lambda: k_loc <= q_loc
    strictly_upper = lambda: k_loc > q_loc

    def step(kt, mask, k_ref, v_ref, q_lo, m_scr, acc_scr):
        start = pl.multiple_of(kt * tq, tq)
        s = _dot(qa_scr[:, q_lo:], k_ref[:, pl.ds(start, tq)])
        if mask is not None:
            s = jnp.where(mask(), s, NEG)
        m_prev = m_scr[...]
        m_new = jnp.maximum(m_prev, jnp.max(s, axis=-1, keepdims=True))
        alpha = jnp.exp(m_prev - m_new)
        p = jnp.concatenate([jnp.exp(s[:, j * LANES:(j + 1) * LANES] - m_new) for j in range(tq // LANES)], axis=-1)
        acc_scr[...] = alpha * acc_scr[...] + _dot(p.astype(BF16), v_ref[pl.ds(start, tq), :])
        m_scr[...] = m_new

    sel = functools.partial(step, k_ref=ksa_ref, v_ref=vs_ref, q_lo=0, m_scr=ms_scr, acc_scr=as_scr)
    win = functools.partial(step, k_ref=kw_ref, v_ref=vw_ref, q_lo=LANES, m_scr=mw_scr, acc_scr=aw_scr)

    for m_scr, acc_scr in ((ms_scr, as_scr), (mw_scr, aw_scr)):
        m_scr[...] = jnp.full(m_scr.shape, NEG, F32)
        acc_scr[...] = jnp.zeros(acc_scr.shape, F32)

    def sel_unrolled(kp, carry):
        for u in range(unroll):
            sel(unroll * kp + u, None)
        return carry

    def sel_single(kt, carry):
        sel(kt, None)
        return carry

    lax.fori_loop(0, i // unroll, sel_unrolled, 0)
    lax.fori_loop((i // unroll) * unroll, i, sel_single, 0)

    @pl.when(i >= 2)
    def _():
        win(i - 2, strictly_upper)
        sel(i, causal)
        win(i - 1, None)
        win(i, causal)

    @pl.when(i < 2)
    def _():
        sel(i, causal)

        @pl.when(i == 1)
        def _():
            win(0, None)

        win(i, causal)

    acc_s = as_scr[...]
    o_sel = acc_s[:, :HEAD_DIM] / acc_s[:, HEAD_DIM:HEAD_DIM + 1]
    acc_w = aw_scr[...]
    o_win = acc_w[:, :HEAD_DIM] / acc_w[:, HEAD_DIM:HEAD_DIM + 1]

    gates = jax.nn.sigmoid(ng_ref[...].astype(F32))
    oc = oc_ref[...].astype(F32)
    lane = lax.broadcasted_iota(I32, gates.shape, 1)
    for h in range(HG):
        c0 = (g * HG + h) * 3
        gc = jnp.sum(jnp.where(lane == c0, gates, 0.0), axis=-1, keepdims=True)
        gs = jnp.sum(jnp.where(lane == c0 + 1, gates, 0.0), axis=-1, keepdims=True)
        gw = jnp.sum(jnp.where(lane == c0 + 2, gates, 0.0), axis=-1, keepdims=True)
        rows = slice(h * tq, (h + 1) * tq)
        cols = slice(h * HEAD_DIM, (h + 1) * HEAD_DIM)
        o_ref[:, cols] = (gc * oc[:, cols] + gs * o_sel[rows, :] + gw * o_win[rows, :]).astype(BF16)


def nsa_attention(proj, selbias, ksa_t, vs_aug, kw_t, vw_aug, oc, B, S, q_col, ng_col, tq=ATT_TILE, unroll=4):
    G = N_KV_GROUPS
    nt = S // tq
    gw = HEADS_PER_GROUP * HEAD_DIM
    assert WINDOW == 2 * tq and tq & (tq - 1) == 0
    R = HEADS_PER_GROUP * tq
    per_group = lambda rows, cols: pl.BlockSpec((None, None, rows, cols), lambda b, g, i: (b, g, 0, 0))
    return pl.pallas_call(
        functools.partial(_nsa_kernel, tq=tq, unroll=unroll),
        grid=(B, G, nt),
        in_specs=[pl.BlockSpec((tq, gw), lambda b, g, i: (b * nt + i, q_col // gw + g)),
                  pl.BlockSpec((None, None, tq, LANES), lambda b, g, i: (b, g, i, 0)),
                  per_group(LANES + HEAD_DIM, S), per_group(S, LANES), per_group(HEAD_DIM, S), per_group(S, LANES),
                  pl.BlockSpec((tq, gw), lambda b, g, i: (b * nt + i, g)),
                  pl.BlockSpec((tq, LANES), lambda b, g, i: (b * nt + i, ng_col // LANES))],
        out_specs=pl.BlockSpec((tq, gw), lambda b, g, i: (b * nt + i, g)),
        out_shape=jax.ShapeDtypeStruct((B * S, G * gw), BF16),
        scratch_shapes=[pltpu.VMEM((R, LANES + HEAD_DIM), BF16),
                        pltpu.VMEM((R, LANES), F32), pltpu.VMEM((R, LANES), F32),
                        pltpu.VMEM((R, LANES), F32), pltpu.VMEM((R, LANES), F32)],
        compiler_params=_params("arbitrary", "arbitrary", "arbitrary"),
        name="nsa_attn",
    )(proj, selbias, ksa_t, vs_aug, kw_t, vw_aug, oc, proj)


def _route(logits, rbias):
    tm, E = logits.shape
    gsz = E // N_GROUPS
    scores = jax.nn.sigmoid(logits)
    sfc = scores + rbias
    lane = lax.broadcasted_iota(I32, (tm, E), 1)
    lane_f = lane.astype(F32)
    grp = lane >> (gsz.bit_length() - 1)
    gscore = []
    for g in range(N_GROUPS):
        v = jnp.where(grp == g, sfc, -jnp.inf)
        m1 = jnp.max(v, axis=-1, keepdims=True)
        cnt = jnp.sum(jnp.where(v == m1, 1.0, 0.0), axis=-1, keepdims=True)
        m2 = jnp.max(jnp.where(v < m1, v, -jnp.inf), axis=-1, keepdims=True)
        gscore.append(m1 + jnp.where(cnt >= 2.0, m1, m2))
    emask = jnp.zeros((tm, E), jnp.bool_)
    for g in range(N_GROUPS):
        rank = jnp.zeros((tm, 1), F32)
        for o in range(N_GROUPS):
            if o == g:
                continue
            ahead = (gscore[o] > gscore[g]) | ((gscore[o] == gscore[g]) & (o < g))
            rank = rank + jnp.where(ahead, 1.0, 0.0)
        emask = emask | ((grp == g) & (rank < float(TOPK_GROUPS)))
    work = jnp.where(emask, sfc, NEG)
    out_lane = lax.broadcasted_iota(I32, (tm, LANES), 1)
    idx_out = jnp.zeros((tm, LANES), F32)
    w_out = jnp.zeros((tm, LANES), F32)
    wsum = jnp.zeros((tm, 1), F32)
    for r in range(TOP_K):
        m = jnp.max(work, axis=-1, keepdims=True)
        first = jnp.min(jnp.where(work == m, lane_f, float(E)), axis=-1, keepdims=True)
        pick = lane_f == first
        w = jnp.sum(jnp.where(pick, scores, 0.0), axis=-1, keepdims=True)
        idx_out = jnp.where(out_lane == r, first, idx_out)
        w_out = jnp.where(out_lane == r, w, w_out)
        wsum = wsum + w
        work = jnp.where(pick, -jnp.inf, work)
    return idx_out.astype(I32), w_out / wsum * ROUTED_SCALE


def _out_proj_kernel(o_ref, za_ref, mb_ref, x_ref, g1_ref, n2_ref, sc_ref, sh_ref, wn_ref, wo_ref,
                     wrh_ref, wrl_ref, rb_ref, x1_ref, h2_ref, idx_ref, wt_ref):
    yb = _dot(o_ref[...], wn_ref[...])
    z = za_ref[...].astype(F32) + jax.nn.sigmoid(mb_ref[...].astype(F32)) * yb
    mix = _dot(z.astype(BF16), wo_ref[...])
    x1 = x_ref[...] + g1_ref[...] * mix
    x1_ref[...] = x1
    h2 = _modulated_rmsnorm(x1, n2_ref[...], sc_ref[...], sh_ref[...])
    h2_ref[...] = h2
    hi, lo = _split_bf16(h2)
    logits = _dot(hi, wrh_ref[...]) + (_dot(lo, wrh_ref[...]) + _dot(hi, wrl_ref[...]))
    idx, wts = _route(logits, rb_ref[...])
    idx_ref[...] = idx
    wt_ref[...] = wts


def out_proj(o, za, proj, x2, g1, norm_g, sc, sh, w_nsa_out, w_out, w_router, router_bias, seq, mb_col, tm=256):
    T, D = x2.shape
    E = w_router.shape[1]
    per_b = seq // tm
    wr_hi, wr_lo = _split_bf16(w_router)
    tok = lambda i: (i, 0)
    const = lambda i: (0, 0)
    per_batch = pl.BlockSpec((None, 1, D), lambda i: (i // per_b, 0, 0))
    return pl.pallas_call(
        _out_proj_kernel,
        grid=(T // tm,),
        in_specs=[pl.BlockSpec((tm, D), tok), pl.BlockSpec((tm, D), tok),
                  pl.BlockSpec((tm, D), lambda i: (i, mb_col // D)),
                  pl.BlockSpec((tm, D), tok), per_batch,
                  pl.BlockSpec((1, D), const), per_batch, per_batch,
                  pl.BlockSpec((D, D), const), pl.BlockSpec((D, D), const),
                  pl.BlockSpec((D, E), const), pl.BlockSpec((D, E), const), pl.BlockSpec((1, E), const)],
        out_specs=[pl.BlockSpec((tm, D), tok), pl.BlockSpec((tm, D), tok),
                   pl.BlockSpec((tm, LANES), tok), pl.BlockSpec((tm, LANES), tok)],
        out_shape=[jax.ShapeDtypeStruct((T, D), F32), jax.ShapeDtypeStruct((T, D), F32),
                   jax.ShapeDtypeStruct((T, LANES), I32), jax.ShapeDtypeStruct((T, LANES), F32)],
        compiler_params=_params("arbitrary"),
        name="out_proj",
    )(o, za, proj, x2, g1[:, None, :], norm_g.reshape(1, D), sc[:, None, :], sh[:, None, :],
      w_nsa_out.astype(BF16), w_out.astype(BF16), wr_hi, wr_lo, router_bias.reshape(1, E))


def _experts_kernel(blk_ref, ex_ref, lo_ref, hi_ref, first_ref, last_ref, ni_ref,
                    ord_hbm, h_hbm, wg_ref, wu_ref, wd_ref, y_hbm,
                    ord_smem, x_buf, y_buf, ord_sem, row_sem, out_sem, *, bm, nblk, n_tok):
    w = pl.program_id(0)
    n_items = ni_ref[0]

    def ord_copy(b, s):
        return pltpu.make_async_copy(ord_hbm.at[b], ord_smem.at[s], ord_sem.at[s])

    def gather_row(s, r, a):
        return pltpu.make_async_copy(h_hbm.at[pl.ds(a >> (TOP_K.bit_length() - 1), 1), :],
                                     x_buf.at[s, pl.ds(r, 1), :], row_sem.at[s])

    def scatter_row(s, r, a):
        dst = (a & (TOP_K - 1)) * n_tok + (a >> (TOP_K.bit_length() - 1))
        return pltpu.make_async_copy(y_buf.at[s, pl.ds(r, 1), :], y_hbm.at[pl.ds(dst, 1), :], out_sem.at[s])

    def issue(make_copy, s):
        for r in range(bm):
            make_copy(s, r, ord_smem[s, r]).start(priority=r % 2)

    def wait_gather(s):
        pltpu.make_async_copy(h_hbm.at[pl.ds(0, bm), :], x_buf.at[s], row_sem.at[s]).wait()

    def wait_scatter(s):
        pltpu.make_async_copy(y_buf.at[s], y_hbm.at[pl.ds(0, bm), :], out_sem.at[s]).wait()

    @pl.when(w < n_items)
    def _():
        b = blk_ref[w]
        s = b % 2
        is_first = first_ref[w] == 1
        is_last = last_ref[w] == 1

        @pl.when(w == 0)
        def _():
            ord_copy(0, 0).start()
            ord_copy(0, 0).wait()
            issue(gather_row, 0)
            if nblk > 1:
                ord_copy(1, 1).start()

        @pl.when(is_first)
        def _():
            @pl.when(b + 1 < nblk)
            def _():
                ord_copy(b + 1, 1 - s).wait()
                issue(gather_row, 1 - s)

            wait_gather(s)

            @pl.when(b >= 2)
            def _():
                wait_scatter(s)

        x = x_buf[s].astype(BF16)
        gate = _dot(x, wg_ref[...])
        up = _dot(x, wu_ref[...])
        hid = gate * jax.nn.sigmoid(gate) * up
        y = _dot(hid.astype(BF16), wd_ref[...])
        rows = lax.broadcasted_iota(I32, (bm, 1), 0)
        mine = (rows >= lo_ref[w]) & (rows < hi_ref[w])

        @pl.when(is_first)
        def _():
            y_buf[s] = jnp.where(mine, y, 0.0)

        @pl.when(jnp.logical_not(is_first))
        def _():
            y_buf[s] = jnp.where(mine, y, y_buf[s])

        @pl.when(is_last)
        def _():
            issue(scatter_row, s)

            @pl.when(b + 2 < nblk)
            def _():
                ord_copy(b + 2, s).start()

        @pl.when(w == n_items - 1)
        def _():
            wait_scatter(s)
            if nblk > 1:
                wait_scatter(1 - s)


def routed_experts(h2, order_blocks, plan, w_gate, w_up, w_down, bm=EXPERT_BLOCK):
    T, D = h2.shape
    nblk = order_blocks.shape[0]
    E, _, DE = w_gate.shape
    n_work = plan[0].shape[0]
    by_expert = lambda w, blk, ex, *_: (ex[w], 0, 0)
    grid_spec = pltpu.PrefetchScalarGridSpec(
        num_scalar_prefetch=len(plan),
        grid=(n_work,),
        in_specs=[pl.BlockSpec(memory_space=pl.ANY),
                  pl.BlockSpec(memory_space=pl.ANY),
                  pl.BlockSpec((None, D, DE), by_expert),
                  pl.BlockSpec((None, D, DE), by_expert),
                  pl.BlockSpec((None, DE, D), by_expert)],
        out_specs=pl.BlockSpec(memory_space=pl.ANY),
        scratch_shapes=[pltpu.SMEM((2, bm), I32), pltpu.VMEM((2, bm, D), F32), pltpu.VMEM((2, bm, D), F32),
                        pltpu.SemaphoreType.DMA((2,)), pltpu.SemaphoreType.DMA((2,)), pltpu.SemaphoreType.DMA((2,))],
    )
    return pl.pallas_call(
        functools.partial(_experts_kernel, bm=bm, nblk=nblk, n_tok=T),
        grid_spec=grid_spec,
        out_shape=jax.ShapeDtypeStruct((nblk * bm, D), F32),
        compiler_params=_params("arbitrary"),
        name="experts",
    )(*plan, order_blocks, h2, w_gate, w_up, w_down)


def _combine_kernel(y8_ref, h2_ref, x1_ref, wt_ref, g2_ref, fg_ref, wsg_ref, wsu_ref, wsd_ref, o_ref):
    h2 = h2_ref[...].astype(BF16)
    gate = _dot(h2, wsg_ref[...])
    up = _dot(h2, wsu_ref[...])
    y = _dot((gate * jax.nn.sigmoid(gate) * up).astype(BF16), wsd_ref[...])
    wt = wt_ref[...]
    for k in range(TOP_K):
        y = y + wt[:, k:k + 1] * y8_ref[k]
    x2 = x1_ref[...] + g2_ref[...] * y
    o_ref[...] = x2 * lax.rsqrt(jnp.mean(x2 * x2, axis=-1, keepdims=True) + EPS) * fg_ref[...]


def combine(y8, h2, x1, wts, g2, final_g, w_sh_gate, w_sh_up, w_sh_down, seq, tj=COMBINE_TILE):
    T, D = x1.shape
    DS = w_sh_gate.shape[1]
    per_b = seq // tj
    tok = lambda i: (i, 0)
    const = lambda i: (0, 0)
    return pl.pallas_call(
        _combine_kernel,
        grid=(T // tj,),
        in_specs=[pl.BlockSpec((TOP_K, tj, D), lambda i: (0, i, 0)),
                  pl.BlockSpec((tj, D), tok), pl.BlockSpec((tj, D), tok), pl.BlockSpec((tj, LANES), tok),
                  pl.BlockSpec((None, 1, D), lambda i: (i // per_b, 0, 0)),
                  pl.BlockSpec((1, D), const),
                  pl.BlockSpec((D, DS), const), pl.BlockSpec((D, DS), const), pl.BlockSpec((DS, D), const)],
        out_specs=pl.BlockSpec((tj, D), tok),
        out_shape=jax.ShapeDtypeStruct((T, D), F32),
        compiler_params=_params("arbitrary"),
        name="combine",
    )(y8, h2, x1, wts, g2[:, None, :], final_g.reshape(1, D),
      w_sh_gate.astype(BF16), w_sh_up.astype(BF16), w_sh_down.astype(BF16))


def _dispatch_plan(idx, T, bm):
    A = T * TOP_K
    assert A % bm == 0
    nblk = A // bm
    E = N_EXPERTS
    sorted_e, order = lax.sort((idx.reshape(A), jnp.arange(A, dtype=I32)), num_keys=1)
    bounds = jnp.searchsorted(sorted_e, jnp.arange(E + 1, dtype=I32), side='left').astype(I32)
    start, end = bounds[:-1], bounds[1:]
    first_blk = start // bm
    n_it = jnp.where(end > start, (end - 1) // bm - first_blk + 1, 0)
    it_end = jnp.cumsum(n_it)
    it_start = it_end - n_it
    n_items = it_end[-1]
    n_work = nblk + E - 1
    w = jnp.arange(n_work, dtype=I32)
    wc = jnp.minimum(w, n_items - 1)
    ex = jnp.searchsorted(it_end, wc, side='right').astype(I32)
    blk = first_blk[ex] + wc - it_start[ex]
    lo = jnp.maximum(start[ex], blk * bm) - blk * bm
    hi = jnp.minimum(end[ex], (blk + 1) * bm) - blk * bm
    first = jnp.concatenate([jnp.ones((1,), I32), (blk[1:] != blk[:-1]).astype(I32)])
    last = jnp.concatenate([(blk[1:] != blk[:-1]).astype(I32), jnp.ones((1,), I32)])
    last = jnp.where(w == n_items - 1, 1, last)
    plan = (blk.astype(I32), ex, lo.astype(I32), hi.astype(I32), first, last, n_items.astype(I32).reshape(1))
    return order.reshape(nblk, bm), plan


def _layer(x, c, w_ada, b_ada, norm1_g, w_in, b_in, conv_w, conv_b, conv_ln_g, conv_ln_b, w_conv_out, b_conv_out,
           cmp_pe_k, cmp_w1_k, cmp_w2_k, cmp_pe_v, cmp_w1_v, cmp_w2_v, w_nsa_out, w_out, norm2_g, w_router,
           router_bias, w_sh_gate, w_sh_up, w_sh_down, w_gate, w_up, w_down, final_g):
    B, S, D = x.shape
    T = B * S
    G, dh = N_KV_GROUPS, HEAD_DIM
    n_q = N_HEADS * dh
    n_kv = 3 * 2 * G * dh
    n_ng = 3 * N_HEADS
    mod = ada_modulation(c, w_ada, b_ada)
    sh1, sc1, g1, sh2, sc2, g2 = jnp.split(mod, 6, axis=-1)

    c_q = 2 * D_CONV
    c_kv = c_q + n_q
    c_ng = c_kv + n_kv
    c_mg = c_ng + n_ng
    pad = LANES - n_ng
    w_r = jnp.concatenate([w_in[:, :c_kv], w_in[:, c_mg:], w_in[:, c_kv:c_mg], jnp.zeros((D, pad), F32)], axis=1)
    b_r = jnp.concatenate([b_in[:c_kv], b_in[c_mg:], b_in[c_kv:c_mg], jnp.zeros((pad,), F32)])
    col_merge = c_kv
    col_kv = col_merge + 2 * D
    col_ng = col_kv + n_kv
    x2 = x.reshape(T, D)
    proj = in_proj(x2, norm1_g, sc1, sh1, w_r.astype(BF16), b_r, S)

    za = conformer_conv(proj, B, S, D, col_merge, conv_w, conv_b, conv_ln_g, conv_ln_b, w_conv_out, b_conv_out)

    kv = proj[:, col_kv:col_kv + n_kv].reshape(B, S, 3, 2, G, dh)
    NC = S // CMP_STRIDE
    zc = kv[:, :, 0].reshape(B, NC, CMP_STRIDE, 2, G, dh).transpose(0, 3, 4, 1, 2, 5).reshape(B, 2, G, NC, CMP_STRIDE * dh)
    kvc = compress_kv(zc, jnp.stack([cmp_pe_k, cmp_pe_v]), jnp.stack([cmp_w1_k, cmp_w1_v]),
                      jnp.stack([cmp_w2_k, cmp_w2_v]))

    cs = CMP_STRIDE * jnp.arange(NC)[:, None]
    ss = SEL_BLOCK * jnp.arange(LANES)[None, :]
    overlap = ((cs <= ss + SEL_BLOCK - 1) & (cs + CMP_BLOCK - 1 >= ss) & (jnp.arange(NC)[:, None] < NC - 1)
               & (jnp.arange(LANES)[None, :] < S // SEL_BLOCK)).astype(BF16)
    oc, selbias = cmp_attention(proj, kvc[:, 0].transpose(0, 1, 3, 2), kvc[:, 1], overlap, B, S, c_q)

    keys_t = lambda br: kv[:, :, br, 0].transpose(0, 2, 3, 1)
    ones_col = jnp.concatenate([jnp.ones((B, G, S, 1), BF16), jnp.zeros((B, G, S, LANES - dh - 1), BF16)], axis=-1)
    vals_aug = lambda br: jnp.concatenate([kv[:, :, br, 1].transpose(0, 2, 1, 3), ones_col], axis=-1)
    onehot_t = (jnp.arange(LANES)[:, None] == jnp.arange(S)[None, :] // SEL_BLOCK).astype(BF16)
    ksa_t = jnp.concatenate([jnp.broadcast_to(onehot_t, (B, G, LANES, S)), keys_t(1)], axis=2)
    o = nsa_attention(proj, selbias, ksa_t, vals_aug(1), keys_t(2), vals_aug(2), oc, B, S, c_q, col_ng)

    x1, h2, idx, wts = out_proj(o, za, proj, x2, g1, norm2_g, sc2, sh2, w_nsa_out, w_out, w_router, router_bias,
                                S, col_merge + D)

    order_blocks, plan = _dispatch_plan(idx[:, :TOP_K], T, EXPERT_BLOCK)
    y8 = routed_experts(h2, order_blocks, plan, w_gate.astype(BF16), w_up.astype(BF16), w_down.astype(BF16))
    out = combine(y8.reshape(TOP_K, T, D), h2, x1, wts, g2, final_g, w_sh_gate, w_sh_up, w_sh_down, S)
    return out.reshape(B, S, D)


def kernel(x, c, w_ada, b_ada, norm1_g, w_in, b_in, conv_w, conv_b, conv_ln_g, conv_ln_b, w_conv_out, b_conv_out, cmp_pe_k, cmp_w1_k, cmp_w2_k, cmp_pe_v, cmp_w1_v, cmp_w2_v, w_nsa_out, w_out, norm2_g, w_router, router_bias, w_sh_gate, w_sh_up, w_sh_down, w_gate, w_up, w_down, final_g):
    assert w_ada.shape[0] == 1, "single-layer block"
    return _layer(x, c, w_ada[0], b_ada[0], norm1_g[0], w_in[0], b_in[0], conv_w[0], conv_b[0], conv_ln_g[0],
                  conv_ln_b[0], w_conv_out[0], b_conv_out[0], cmp_pe_k[0], cmp_w1_k[0], cmp_w2_k[0], cmp_pe_v[0],
                  cmp_w1_v[0], cmp_w2_v[0], w_nsa_out[0], w_out[0], norm2_g[0], w_router[0], router_bias[0],
                  w_sh_gate[0], w_sh_up[0], w_sh_down[0], w_gate[0], w_up[0], w_down[0], final_g)
```

```python
import functools

import jax
import jax.numpy as jnp
from jax import lax
from jax.experimental import pallas as pl
from jax.experimental.pallas import tpu as pltpu

BF16 = jnp.bfloat16
F32 = jnp.float32
I32 = jnp.int32

D_CONV = 512
CONV_K = 31
N_HEADS = 16
N_KV_GROUPS = 4
HEADS_PER_GROUP = 4
HEAD_DIM = 64
CMP_BLOCK = 32
CMP_STRIDE = 16
SEL_BLOCK = 64
SEL_TOPK = 16
WINDOW = 512
FORCED_SCORE = 1e4
N_EXPERTS = 256
TOP_K = 8
N_GROUPS = 8
TOPK_GROUPS = 4
ROUTED_SCALE = 2.5
EPS = 1e-6
NEG = -1e30
MASK_BIAS = -1e9
LANES = 128
HALO = 32
ATT_TILE = 256
EXPERT_BLOCK = 256
COMBINE_TILE = 256
VMEM_LIMIT = 56 * 1024 * 1024


def _params(*sem):
    return pltpu.CompilerParams(dimension_semantics=sem, vmem_limit_bytes=VMEM_LIMIT)


def _dot(a, b):
    return jnp.dot(a, b, preferred_element_type=F32)


def _split_bf16(x):
    hi = x.astype(BF16)
    lo = (x - hi.astype(F32)).astype(BF16)
    return hi, lo


def _ada_kernel(c_ref, w_ref, b_ref, o_ref):
    c = c_ref[...]
    a = c * jax.nn.sigmoid(c)
    o_ref[...] = _dot(a.astype(BF16), w_ref[...].astype(BF16)) + b_ref[...]


def ada_modulation(c, w_ada, b_ada):
    B, D = c.shape
    N = w_ada.shape[1]
    rows = 8
    c8 = jnp.zeros((rows, D), F32).at[:B].set(c)
    tn = 1024
    out = pl.pallas_call(
        _ada_kernel,
        grid=(N // tn,),
        in_specs=[pl.BlockSpec((rows, D), lambda j: (0, 0)),
                  pl.BlockSpec((D, tn), lambda j: (0, j)),
                  pl.BlockSpec((1, tn), lambda j: (0, j))],
        out_specs=pl.BlockSpec((rows, tn), lambda j: (0, j)),
        out_shape=jax.ShapeDtypeStruct((rows, N), F32),
        compiler_params=_params("arbitrary"),
        name="ada",
    )(c8, w_ada, b_ada.reshape(1, N))
    return out[:B]


def _modulated_rmsnorm(x, g, sc, sh):
    y = x * lax.rsqrt(jnp.mean(x * x, axis=-1, keepdims=True) + EPS)
    return (y * g) * (1.0 + sc) + sh


def _in_proj_kernel(x_ref, g_ref, sc_ref, sh_ref, w_ref, b_ref, o_ref, h_scr):
    @pl.when(pl.program_id(1) == 0)
    def _():
        h = _modulated_rmsnorm(x_ref[...], g_ref[...], sc_ref[...], sh_ref[...])
        h_scr[...] = h.astype(BF16)

    o_ref[...] = (_dot(h_scr[...], w_ref[...]) + b_ref[...]).astype(BF16)


def in_proj(x2, norm_g, sc, sh, w, b, seq, tm=1024, tn=1152):
    T, D = x2.shape
    NP = w.shape[1]
    per_b = seq // tm
    return pl.pallas_call(
        _in_proj_kernel,
        grid=(T // tm, NP // tn),
        in_specs=[pl.BlockSpec((tm, D), lambda i, j: (i, 0)),
                  pl.BlockSpec((1, D), lambda i, j: (0, 0)),
                  pl.BlockSpec((None, 1, D), lambda i, j: (i // per_b, 0, 0)),
                  pl.BlockSpec((None, 1, D), lambda i, j: (i // per_b, 0, 0)),
                  pl.BlockSpec((D, tn), lambda i, j: (0, j)),
                  pl.BlockSpec((1, tn), lambda i, j: (0, j))],
        out_specs=pl.BlockSpec((tm, tn), lambda i, j: (i, j)),
        out_shape=jax.ShapeDtypeStruct((T, NP), BF16),
        scratch_shapes=[pltpu.VMEM((tm, D), BF16)],
        compiler_params=_params("arbitrary", "arbitrary"),
        name="in_proj",
    )(x2, norm_g.reshape(1, D), sc[:, None, :], sh[:, None, :], w, b.reshape(1, NP))


def _conv_kernel(a_ref, gt_ref, pa_ref, pg_ref, mg_ref, cw_ref, cb_ref, lg_ref, lb_ref, wo_ref, bo_ref,
                 o_ref, u_scr, c_scr, *, ts, chunk):
    i = pl.program_id(1)
    a = a_ref[...].astype(F32)
    u_scr[HALO:, :] = a * jax.nn.sigmoid(gt_ref[...].astype(F32))
    pa = pa_ref[...].astype(F32)
    prev = pa * jax.nn.sigmoid(pg_ref[...].astype(F32))
    u_scr[:HALO, :] = jnp.where(i > 0, prev, 0.0)
    off = HALO - (CONV_K - 1)
    for r0 in range(0, ts, chunk):
        acc = jnp.broadcast_to(cb_ref[...], (chunk, D_CONV))
        for k in range(CONV_K):
            acc = acc + cw_ref[k:k + 1, :] * u_scr[r0 + off + k:r0 + off + k + chunk, :]
        c_scr[r0:r0 + chunk, :] = acc
    v = c_scr[...]
    mu = jnp.mean(v, axis=-1, keepdims=True)
    var = jnp.mean(jnp.square(v - mu), axis=-1, keepdims=True)
    y = (v - mu) * lax.rsqrt(var + EPS) * lg_ref[...] + lb_ref[...]
    y = y * jax.nn.sigmoid(y)
    ya = _dot(y.astype(BF16), wo_ref[...]) + bo_ref[...]
    o_ref[...] = (jax.nn.sigmoid(mg_ref[...].astype(F32)) * ya).astype(BF16)


def conformer_conv(proj, B, S, D, merge_col, conv_w, conv_b, ln_g, ln_b, w_o, b_o, ts=512, chunk=64):
    T = B * S
    nt = S // ts
    hb = ts // HALO
    prev_idx = lambda b, i: (jnp.maximum((b * nt + i) * hb - 1, 0), 0)
    prev_idx_g = lambda b, i: (jnp.maximum((b * nt + i) * hb - 1, 0), 1)
    row = lambda v: v.reshape(1, -1)
    return pl.pallas_call(
        functools.partial(_conv_kernel, ts=ts, chunk=chunk),
        grid=(B, nt),
        in_specs=[pl.BlockSpec((ts, D_CONV), lambda b, i: (b * nt + i, 0)),
                  pl.BlockSpec((ts, D_CONV), lambda b, i: (b * nt + i, 1)),
                  pl.BlockSpec((HALO, D_CONV), prev_idx),
                  pl.BlockSpec((HALO, D_CONV), prev_idx_g),
                  pl.BlockSpec((ts, D), lambda b, i: (b * nt + i, merge_col // D)),
                  pl.BlockSpec((CONV_K, D_CONV), lambda b, i: (0, 0)),
                  pl.BlockSpec((1, D_CONV), lambda b, i: (0, 0)),
                  pl.BlockSpec((1, D_CONV), lambda b, i: (0, 0)),
                  pl.BlockSpec((1, D_CONV), lambda b, i: (0, 0)),
                  pl.BlockSpec((D_CONV, D), lambda b, i: (0, 0)),
                  pl.BlockSpec((1, D), lambda b, i: (0, 0))],
        out_specs=pl.BlockSpec((ts, D), lambda b, i: (b * nt + i, 0)),
        out_shape=jax.ShapeDtypeStruct((T, D), BF16),
        scratch_shapes=[pltpu.VMEM((ts + HALO, D_CONV), F32), pltpu.VMEM((ts, D_CONV), F32)],
        compiler_params=_params("arbitrary", "arbitrary"),
        name="conv",
    )(proj, proj, proj, proj, proj, conv_w, row(conv_b), row(ln_g), row(ln_b), w_o.astype(BF16), row(b_o))


def _compress_kernel(z_ref, pe_ref, w1_ref, w2_ref, o_ref, *, nc):
    half = w1_ref.shape[0] // 2
    z = z_ref[...]
    first = _dot(z, w1_ref[:half, :])
    second = _dot(z, w1_ref[half:, :])
    bias = _dot(pe_ref[...], w1_ref[...])[0:1, :]
    hid = first + pltpu.roll(second, nc - 1, 0) + bias
    act = jax.nn.gelu(hid)
    out = _dot(act.astype(BF16), w2_ref[...])
    rows = lax.broadcasted_iota(I32, out.shape, 0)
    o_ref[...] = jnp.where(rows < nc - 1, out, 0.0).astype(BF16)


def compress_kv(zc, pe, w1, w2):
    B, _, G, NC, W = zc.shape
    H = w1.shape[-1]
    pe8 = jnp.broadcast_to(pe.reshape(2, 1, 2 * W), (2, 8, 2 * W)).astype(BF16)
    return pl.pallas_call(
        functools.partial(_compress_kernel, nc=NC),
        grid=(B, 2, G),
        in_specs=[pl.BlockSpec((None, None, None, NC, W), lambda b, s, g: (b, s, g, 0, 0)),
                  pl.BlockSpec((None, 8, 2 * W), lambda b, s, g: (s, 0, 0)),
                  pl.BlockSpec((None, 2 * W, H), lambda b, s, g: (s, 0, 0)),
                  pl.BlockSpec((None, H, HEAD_DIM), lambda b, s, g: (s, 0, 0))],
        out_specs=pl.BlockSpec((None, None, None, NC, HEAD_DIM), lambda b, s, g: (b, s, g, 0, 0)),
        out_shape=jax.ShapeDtypeStruct((B, 2, G, NC, HEAD_DIM), BF16),
        compiler_params=_params("arbitrary", "arbitrary", "arbitrary"),
        name="compress",
    )(zc, pe8, w1.astype(BF16), w2.astype(BF16))


def _cmp_attn_kernel(q_ref, kct_ref, vc_ref, ov_ref, oc_ref, sb_ref, qa_scr, *, tq, ns):
    i = pl.program_id(2)
    t0 = i * tq
    HG = HEADS_PER_GROUP
    R = HG * tq
    nc = kct_ref.shape[1]
    q = q_ref[...] * (HEAD_DIM ** -0.5)
    for h in range(HG):
        qa_scr[h * tq:(h + 1) * tq, :] = q[:, h * HEAD_DIM:(h + 1) * HEAD_DIM]
    t_ids = t0 + (lax.broadcasted_iota(I32, (R, nc), 0) & (tq - 1))
    n_ids = lax.broadcasted_iota(I32, (R, nc), 1)
    valid = (CMP_STRIDE * n_ids + (CMP_BLOCK - 1)) <= t_ids
    s = jnp.where(valid, _dot(qa_scr[...], kct_ref[...]), NEG)
    m = jnp.max(s, axis=-1, keepdims=True)
    e = jnp.where(valid, jnp.exp(s - m), 0.0)
    den = jnp.sum(e, axis=-1, keepdims=True)
    p = e * jnp.where(den > 0.0, 1.0 / den, 0.0)
    o = _dot(p.astype(BF16), vc_ref[...])
    for h in range(HG):
        oc_ref[:, h * HEAD_DIM:(h + 1) * HEAD_DIM] = o[h * tq:(h + 1) * tq, :].astype(BF16)
    psum = p[:tq]
    for h in range(1, HG):
        psum = psum + p[h * tq:(h + 1) * tq]
    ps_hi, ps_lo = _split_bf16(psum)
    imp = _dot(ps_hi, ov_ref[...]) + _dot(ps_lo, ov_ref[...])
    lane = lax.broadcasted_iota(I32, (tq, LANES), 1)
    jq = (t0 + lax.broadcasted_iota(I32, (tq, LANES), 0)) >> (SEL_BLOCK.bit_length() - 1)
    forced = (lane == 0) | (lane == jq) | (lane == jq - 1)
    imp = jnp.where(forced, FORCED_SCORE, jnp.where(lane <= jq, imp, -1.0))
    work = jnp.where(lane < ns, imp, -jnp.inf).T
    blk_f = lax.broadcasted_iota(I32, (LANES, tq), 0).astype(F32)
    sel = jnp.zeros((LANES, tq), F32)
    for _ in range(SEL_TOPK):
        top = jnp.max(work, axis=0, keepdims=True)
        first = jnp.min(jnp.where(work == top, blk_f, float(LANES)), axis=0, keepdims=True)
        pick = blk_f == first
        sel = jnp.where(pick, 1.0, sel)
        work = jnp.where(pick, -jnp.inf, work)
    sb_ref[...] = jnp.where((sel.T > 0.5) & (lane <= jq), 0.0, MASK_BIAS).astype(BF16)


def cmp_attention(proj, kc_t, vc, overlap, B, S, q_col, tq=ATT_TILE):
    G = N_KV_GROUPS
    NC = vc.shape[2]
    nt = S // tq
    gw = HEADS_PER_GROUP * HEAD_DIM
    ns = S // SEL_BLOCK
    assert SEL_TOPK <= ns <= LANES and tq & (tq - 1) == 0
    return pl.pallas_call(
        functools.partial(_cmp_attn_kernel, tq=tq, ns=ns),
        grid=(B, G, nt),
        in_specs=[pl.BlockSpec((tq, gw), lambda b, g, i: (b * nt + i, q_col // gw + g)),
                  pl.BlockSpec((None, None, HEAD_DIM, NC), lambda b, g, i: (b, g, 0, 0)),
                  pl.BlockSpec((None, None, NC, HEAD_DIM), lambda b, g, i: (b, g, 0, 0)),
                  pl.BlockSpec((NC, LANES), lambda b, g, i: (0, 0))],
        out_specs=[pl.BlockSpec((tq, gw), lambda b, g, i: (b * nt + i, g)),
                   pl.BlockSpec((None, None, tq, LANES), lambda b, g, i: (b, g, i, 0))],
        out_shape=[jax.ShapeDtypeStruct((B * S, G * gw), BF16),
                   jax.ShapeDtypeStruct((B, G, S, LANES), BF16)],
        scratch_shapes=[pltpu.VMEM((HEADS_PER_GROUP * tq, HEAD_DIM), BF16)],
        compiler_params=_params("arbitrary", "arbitrary", "arbitrary"),
        name="cmp_attn",
    )(proj, kc_t, vc, overlap)


def _nsa_kernel(q_ref, sb_ref, ksa_ref, vs_ref, kw_ref, vw_ref, oc_ref, ng_ref, o_ref,
                qa_scr, ms_scr, as_scr, mw_scr, aw_scr, *, tq, unroll):
    g = pl.program_id(1)
    i = pl.program_id(2)
    HG = HEADS_PER_GROUP
    R = HG * tq
    q = q_ref[...] * (HEAD_DIM ** -0.5)
    sb = sb_ref[...]
    for h in range(HG):
        qa_scr[h * tq:(h + 1) * tq, :LANES] = sb
        qa_scr[h * tq:(h + 1) * tq, LANES:] = q[:, h * HEAD_DIM:(h + 1) * HEAD_DIM]

    q_loc = lax.broadcasted_iota(I32, (R, tq), 0) & (tq - 1)
    k_loc = lax.broadcasted_iota(I32, (R, tq), 1)
    causal = lambda: k_loc <= q_loc
    strictly_upper = lambda: k_loc > q_loc

    def step(kt, mask, k_ref, v_ref, q_lo, m_scr, acc_scr):
        start = pl.multiple_of(kt * tq, tq)
        s = _dot(qa_scr[:, q_lo:], k_ref[:, pl.ds(start, tq)])
        if mask is not None:
            s = jnp.where(mask(), s, NEG)
        m_prev = m_scr[...]
        m_new = jnp.maximum(m_prev, jnp.max(s, axis=-1, keepdims=True))
        alpha = jnp.exp(m_prev - m_new)
        p = jnp.concatenate([jnp.exp(s[:, j * LANES:(j + 1) * LANES] - m_new) for j in range(tq // LANES)], axis=-1)
        acc_scr[...] = alpha * acc_scr[...] + _dot(p.astype(BF16), v_ref[pl.ds(start, tq), :])
        m_scr[...] = m_new

    sel = functools.partial(step, k_ref=ksa_ref, v_ref=vs_ref, q_lo=0, m_scr=ms_scr, acc_scr=as_scr)
    win = functools.partial(step, k_ref=kw_ref, v_ref=vw_ref, q_lo=LANES, m_scr=mw_scr, acc_scr=aw_scr)

    for m_scr, acc_scr in ((ms_scr, as_scr), (mw_scr, aw_scr)):
        m_scr[...] = jnp.full(m_scr.shape, NEG, F32)
        acc_scr[...] = jnp.zeros(acc_scr.shape, F32)

    def sel_unrolled(kp, carry):
        for u in range(unroll):
            sel(unroll * kp + u, None)
        return carry

    def sel_single(kt, carry):
        sel(kt, None)
        return carry

    lax.fori_loop(0, i // unroll, sel_unrolled, 0)
    lax.fori_loop((i // unroll) * unroll, i, sel_single, 0)

    @pl.when(i >= 2)
    def _():
        win(i - 2, strictly_upper)
        sel(i, causal)
        win(i - 1, None)
        win(i, causal)

    @pl.when(i < 2)
    def _():
        sel(i, causal)

        @pl.when(i == 1)
        def _():
            win(0, None)

        win(i, causal)

    acc_s = as_scr[...]
    o_sel = acc_s[:, :HEAD_DIM] / acc_s[:, HEAD_DIM:HEAD_DIM + 1]
    acc_w = aw_scr[...]
    o_win = acc_w[:, :HEAD_DIM] / acc_w[:, HEAD_DIM:HEAD_DIM + 1]

    gates = jax.nn.sigmoid(ng_ref[...].astype(F32))
    oc = oc_ref[...].astype(F32)
    lane = lax.broadcasted_iota(I32, gates.shape, 1)
    for h in range(HG):
        c0 = (g * HG + h) * 3
        gc = jnp.sum(jnp.where(lane == c0, gates, 0.0), axis=-1, keepdims=True)
        gs = jnp.sum(jnp.where(lane == c0 + 1, gates, 0.0), axis=-1, keepdims=True)
        gw = jnp.sum(jnp.where(lane == c0 + 2, gates, 0.0), axis=-1, keepdims=True)
        rows = slice(h * tq, (h + 1) * tq)
        cols = slice(h * HEAD_DIM, (h + 1) * HEAD_DIM)
        o_ref[:, cols] = (gc * oc[:, cols] + gs * o_sel[rows, :] + gw * o_win[rows, :]).astype(BF16)


def nsa_attention(proj, selbias, ksa_t, vs_aug, kw_t, vw_aug, oc, B, S, q_col, ng_col, tq=ATT_TILE, unroll=4):
    G = N_KV_GROUPS
    nt = S // tq
    gw = HEADS_PER_GROUP * HEAD_DIM
    assert WINDOW == 2 * tq and tq & (tq - 1) == 0
    R = HEADS_PER_GROUP * tq
    per_group = lambda rows, cols: pl.BlockSpec((None, None, rows, cols), lambda b, g, i: (b, g, 0, 0))
    return pl.pallas_call(
        functools.partial(_nsa_kernel, tq=tq, unroll=unroll),
        grid=(B, G, nt),
        in_specs=[pl.BlockSpec((tq, gw), lambda b, g, i: (b * nt + i, q_col // gw + g)),
                  pl.BlockSpec((None, None, tq, LANES), lambda b, g, i: (b, g, i, 0)),
                  per_group(LANES + HEAD_DIM, S), per_group(S, LANES), per_group(HEAD_DIM, S), per_group(S, LANES),
                  pl.BlockSpec((tq, gw), lambda b, g, i: (b * nt + i, g)),
                  pl.BlockSpec((tq, LANES), lambda b, g, i: (b * nt + i, ng_col // LANES))],
        out_specs=pl.BlockSpec((tq, gw), lambda b, g, i: (b * nt + i, g)),
        out_shape=jax.ShapeDtypeStruct((B * S, G * gw), BF16),
        scratch_shapes=[pltpu.VMEM((R, LANES + HEAD_DIM), BF16),
                        pltpu.VMEM((R, LANES), F32), pltpu.VMEM((R, LANES), F32),
                        pltpu.VMEM((R, LANES), F32), pltpu.VMEM((R, LANES), F32)],
        compiler_params=_params("arbitrary", "arbitrary", "arbitrary"),
        name="nsa_attn",
    )(proj, selbias, ksa_t, vs_aug, kw_t, vw_aug, oc, proj)


def _route(logits, rbias):
    tm, E = logits.shape
    gsz = E // N_GROUPS
    scores = jax.nn.sigmoid(logits)
    sfc = scores + rbias
    lane = lax.broadcasted_iota(I32, (tm, E), 1)
    lane_f = lane.astype(F32)
    grp = lane >> (gsz.bit_length() - 1)
    gscore = []
    for g in range(N_GROUPS):
        v = jnp.where(grp == g, sfc, -jnp.inf)
        m1 = jnp.max(v, axis=-1, keepdims=True)
        cnt = jnp.sum(jnp.where(v == m1, 1.0, 0.0), axis=-1, keepdims=True)
        m2 = jnp.max(jnp.where(v < m1, v, -jnp.inf), axis=-1, keepdims=True)
        gscore.append(m1 + jnp.where(cnt >= 2.0, m1, m2))
    emask = jnp.zeros((tm, E), jnp.bool_)
    for g in range(N_GROUPS):
        rank = jnp.zeros((tm, 1), F32)
        for o in range(N_GROUPS):
            if o == g:
                continue
            ahead = (gscore[o] > gscore[g]) | ((gscore[o] == gscore[g]) & (o < g))
            rank = rank + jnp.where(ahead, 1.0, 0.0)
        emask = emask | ((grp == g) & (rank < float(TOPK_GROUPS)))
    work = jnp.where(emask, sfc, NEG)
    out_lane = lax.broadcasted_iota(I32, (tm, LANES), 1)
    idx_out = jnp.zeros((tm, LANES), F32)
    w_out = jnp.zeros((tm, LANES), F32)
    wsum = jnp.zeros((tm, 1), F32)
    for r in range(TOP_K):
        m = jnp.max(work, axis=-1, keepdims=True)
        first = jnp.min(jnp.where(work == m, lane_f, float(E)), axis=-1, keepdims=True)
        pick = lane_f == first
        w = jnp.sum(jnp.where(pick, scores, 0.0), axis=-1, keepdims=True)
        idx_out = jnp.where(out_lane == r, first, idx_out)
        w_out = jnp.where(out_lane == r, w, w_out)
        wsum = wsum + w
        work = jnp.where(pick, -jnp.inf, work)
    return idx_out.astype(I32), w_out / wsum * ROUTED_SCALE


def _out_proj_kernel(o_ref, za_ref, mb_ref, x_ref, g1_ref, n2_ref, sc_ref, sh_ref, wn_ref, wo_ref,
                     wrh_ref, wrl_ref, rb_ref, x1_ref, h2_ref, idx_ref, wt_ref):
    yb = _dot(o_ref[...], wn_ref[...])
    z = za_ref[...].astype(F32) + jax.nn.sigmoid(mb_ref[...].astype(F32)) * yb
    mix = _dot(z.astype(BF16), wo_ref[...])
    x1 = x_ref[...] + g1_ref[...] * mix
    x1_ref[...] = x1
    h2 = _modulated_rmsnorm(x1, n2_ref[...], sc_ref[...], sh_ref[...])
    h2_ref[...] = h2
    hi, lo = _split_bf16(h2)
    logits = _dot(hi, wrh_ref[...]) + (_dot(lo, wrh_ref[...]) + _dot(hi, wrl_ref[...]))
    idx, wts = _route(logits, rb_ref[...])
    idx_ref[...] = idx
    wt_ref[...] = wts


def out_proj(o, za, proj, x2, g1, norm_g, sc, sh, w_nsa_out, w_out, w_router, router_bias, seq, mb_col, tm=256):
    T, D = x2.shape
    E = w_router.shape[1]
    per_b = seq // tm
    wr_hi, wr_lo = _split_bf16(w_router)
    tok = lambda i: (i, 0)
    const = lambda i: (0, 0)
    per_batch = pl.BlockSpec((None, 1, D), lambda i: (i // per_b, 0, 0))
    return pl.pallas_call(
        _out_proj_kernel,
        grid=(T // tm,),
        in_specs=[pl.BlockSpec((tm, D), tok), pl.BlockSpec((tm, D), tok),
                  pl.BlockSpec((tm, D), lambda i: (i, mb_col // D)),
                  pl.BlockSpec((tm, D), tok), per_batch,
                  pl.BlockSpec((1, D), const), per_batch, per_batch,
                  pl.BlockSpec((D, D), const), pl.BlockSpec((D, D), const),
                  pl.BlockSpec((D, E), const), pl.BlockSpec((D, E), const), pl.BlockSpec((1, E), const)],
        out_specs=[pl.BlockSpec((tm, D), tok), pl.BlockSpec((tm, D), tok),
                   pl.BlockSpec((tm, LANES), tok), pl.BlockSpec((tm, LANES), tok)],
        out_shape=[jax.ShapeDtypeStruct((T, D), F32), jax.ShapeDtypeStruct((T, D), F32),
                   jax.ShapeDtypeStruct((T, LANES), I32), jax.ShapeDtypeStruct((T, LANES), F32)],
        compiler_params=_params("arbitrary"),
        name="out_proj",
    )(o, za, proj, x2, g1[:, None, :], norm_g.reshape(1, D), sc[:, None, :], sh[:, None, :],
      w_nsa_out.astype(BF16), w_out.astype(BF16), wr_hi, wr_lo, router_bias.reshape(1, E))


def _experts_kernel(ex_ref, ni_ref, gidx_hbm, sidx_hbm, h_hbm, wg0, wu0, wd0, wg1, wu1, wd1, y_hbm,
                    g_idx, s_idx, x_buf, y_buf, g_sem, s_sem, row_sem, out_sem, *, bm, dump_row):
    n = pl.program_id(0)
    n_items = ni_ref[0]
    w0 = 2 * n
    w1 = w0 + 1

    def idx_copy(src_hbm, w, smem, sem, s):
        return pltpu.make_async_copy(src_hbm.at[w], smem.at[s], sem.at[s])

    gather_idx = functools.partial(idx_copy, gidx_hbm, smem=g_idx, sem=g_sem)
    scatter_idx = functools.partial(idx_copy, sidx_hbm, smem=s_idx, sem=s_sem)

    def issue_gather(s):
        for r in range(bm):
            pltpu.make_async_copy(h_hbm.at[pl.ds(g_idx[s, r], 1), :], x_buf.at[s, pl.ds(r, 1), :],
                                  row_sem.at[s]).start(priority=r % 2)

    def issue_scatter(s):
        for r in range(bm):
            pltpu.make_async_copy(y_buf.at[s, pl.ds(r, 1), :], y_hbm.at[pl.ds(s_idx[s, r], 1), :],
                                  out_sem.at[s]).start(priority=r % 2)

    def wait_gather(s):
        pltpu.make_async_copy(h_hbm.at[pl.ds(0, bm), :], x_buf.at[s], row_sem.at[s]).wait()

    def wait_scatter(s):
        pltpu.make_async_copy(y_buf.at[s], y_hbm.at[pl.ds(0, bm), :], out_sem.at[s]).wait()

    def expert(s, wg, wu, wd):
        x = x_buf[s].astype(BF16)
        gate = _dot(x, wg[...].astype(BF16))
        up = _dot(x, wu[...].astype(BF16))
        hid = gate * jax.nn.sigmoid(gate) * up
        y_buf[s] = _dot(hid.astype(BF16), wd[...].astype(BF16))

    @pl.when(n == 0)
    def _():
        y_buf[...] = jnp.zeros(y_buf.shape, F32)
        for s in range(2):
            pltpu.make_async_copy(y_buf.at[s], y_hbm.at[pl.ds(dump_row + s * bm, bm), :], out_sem.at[s]).start()
        gather_idx(w=0, s=0).start()
        gather_idx(w=1, s=1).start()
        gather_idx(w=0, s=0).wait()
        issue_gather(0)

    @pl.when(w0 < n_items)
    def _():
        gather_idx(w=w0 + 2, s=0).start()
        scatter_idx(w=w0, s=0).start()
        scatter_idx(w=w1, s=1).start()
        gather_idx(w=w1, s=1).wait()
        issue_gather(1)
        gather_idx(w=w1 + 2, s=1).start()
        wait_gather(0)
        wait_scatter(0)
        expert(0, wg0, wu0, wd0)
        scatter_idx(w=w0, s=0).wait()
        issue_scatter(0)
        gather_idx(w=w0 + 2, s=0).wait()
        issue_gather(0)
        wait_gather(1)
        wait_scatter(1)
        expert(1, wg1, wu1, wd1)
        scatter_idx(w=w1, s=1).wait()
        issue_scatter(1)

        @pl.when(w0 + 2 >= n_items)
        def _():
            wait_gather(0)
            gather_idx(w=w1 + 2, s=1).wait()
            wait_scatter(0)
            wait_scatter(1)


def routed_experts(h2, plan, w_gate, w_up, w_down, n_assign, bm=EXPERT_BLOCK):
    ex, n_items, gidx, sidx = plan
    T, D = h2.shape
    E, _, DE = w_gate.shape
    n_work = ex.shape[0] - 2
    even = lambda n, ex, ni: (ex[2 * n], 0, 0)
    odd = lambda n, ex, ni: (ex[2 * n + 1], 0, 0)
    grid_spec = pltpu.PrefetchScalarGridSpec(
        num_scalar_prefetch=2,
        grid=(n_work // 2,),
        in_specs=[pl.BlockSpec(memory_space=pl.ANY), pl.BlockSpec(memory_space=pl.ANY),
                  pl.BlockSpec(memory_space=pl.ANY),
                  pl.BlockSpec((None, D, DE), even), pl.BlockSpec((None, D, DE), even), pl.BlockSpec((None, DE, D), even),
                  pl.BlockSpec((None, D, DE), odd), pl.BlockSpec((None, D, DE), odd), pl.BlockSpec((None, DE, D), odd)],
        out_specs=pl.BlockSpec(memory_space=pl.ANY),
        scratch_shapes=[pltpu.SMEM((2, bm), I32), pltpu.SMEM((2, bm), I32),
                        pltpu.VMEM((2, bm, D), F32), pltpu.VMEM((2, bm, D), F32),
                        pltpu.SemaphoreType.DMA((2,)), pltpu.SemaphoreType.DMA((2,)),
                        pltpu.SemaphoreType.DMA((2,)), pltpu.SemaphoreType.DMA((2,))],
    )
    return pl.pallas_call(
        functools.partial(_experts_kernel, bm=bm, dump_row=n_assign),
        grid_spec=grid_spec,
        out_shape=jax.ShapeDtypeStruct((n_assign + 2 * bm, D), F32),
        compiler_params=_params("arbitrary"),
        name="experts",
    )(ex, n_items, gidx, sidx, h2, w_gate, w_up, w_down, w_gate, w_up, w_down)


def _combine_kernel(*refs):
    y8_refs = refs[:TOP_K]
    h2_ref, x1_ref, wt_ref, g2_ref, fg_ref, wsg_ref, wsu_ref, wsd_ref, o_ref = refs[TOP_K:]
    h2 = h2_ref[...].astype(BF16)
    gate = _dot(h2, wsg_ref[...])
    up = _dot(h2, wsu_ref[...])
    y = _dot((gate * jax.nn.sigmoid(gate) * up).astype(BF16), wsd_ref[...])
    wt = wt_ref[...]
    for k in range(TOP_K):
        y = y + wt[:, k:k + 1] * y8_refs[k][...]
    x2 = x1_ref[...] + g2_ref[...] * y
    o_ref[...] = x2 * lax.rsqrt(jnp.mean(x2 * x2, axis=-1, keepdims=True) + EPS) * fg_ref[...]


def combine(y8, h2, x1, wts, g2, final_g, w_sh_gate, w_sh_up, w_sh_down, seq, tj=COMBINE_TILE):
    T, D = x1.shape
    DS = w_sh_gate.shape[1]
    per_b = seq // tj
    nt = T // tj
    tok = lambda i: (i, 0)
    const = lambda i: (0, 0)
    slot = lambda k: pl.BlockSpec((tj, D), lambda i: (k * nt + i, 0))
    return pl.pallas_call(
        _combine_kernel,
        grid=(nt,),
        in_specs=[slot(k) for k in range(TOP_K)] + [
                  pl.BlockSpec((tj, D), tok), pl.BlockSpec((tj, D), tok), pl.BlockSpec((tj, LANES), tok),
                  pl.BlockSpec((None, 1, D), lambda i: (i // per_b, 0, 0)),
                  pl.BlockSpec((1, D), const),
                  pl.BlockSpec((D, DS), const), pl.BlockSpec((D, DS), const), pl.BlockSpec((DS, D), const)],
        out_specs=pl.BlockSpec((tj, D), tok),
        out_shape=jax.ShapeDtypeStruct((T, D), F32),
        compiler_params=_params("arbitrary"),
        name="combine",
    )(*([y8] * TOP_K), h2, x1, wts, g2[:, None, :], final_g.reshape(1, D),
      w_sh_gate.astype(BF16), w_sh_up.astype(BF16), w_sh_down.astype(BF16))


def _dispatch_plan(idx, T, bm):
    A = T * TOP_K
    assert A % bm == 0
    nblk = A // bm
    E = N_EXPERTS
    kbits = TOP_K.bit_length() - 1
    sorted_e, order = lax.sort((idx.reshape(A), jnp.arange(A, dtype=I32)), num_keys=1)
    bounds = jnp.searchsorted(sorted_e, jnp.arange(E + 1, dtype=I32), side='left').astype(I32)
    start, end = bounds[:-1], bounds[1:]
    first_blk = start // bm
    n_it = jnp.where(end > start, (end - 1) // bm - first_blk + 1, 0)
    it_end = jnp.cumsum(n_it)
    it_start = it_end - n_it
    n_items = it_end[-1]
    n_work = nblk + E
    w = jnp.arange(n_work + 2, dtype=I32)
    real = w < n_items
    wc = jnp.minimum(w, n_items - 1)
    ex = jnp.searchsorted(it_end, wc, side='right').astype(I32)
    blk = first_blk[ex] + wc - it_start[ex]
    lo = jnp.maximum(start[ex], blk * bm) - blk * bm
    hi = jnp.where(real, jnp.minimum(end[ex], (blk + 1) * bm) - blk * bm, lo)
    rows = order.reshape(nblk, bm)[blk]
    r = jnp.arange(bm, dtype=I32)[None, :]
    mine = (r >= lo[:, None]) & (r < hi[:, None])
    gidx = rows >> kbits
    sidx = jnp.where(mine, (rows & (TOP_K - 1)) * T + gidx, A + (w[:, None] % 2) * bm + r)
    return ex, n_items.astype(I32).reshape(1), gidx.astype(I32), sidx.astype(I32)


def _layer(x, c, w_ada, b_ada, norm1_g, w_in, b_in, conv_w, conv_b, conv_ln_g, conv_ln_b, w_conv_out, b_conv_out,
           cmp_pe_k, cmp_w1_k, cmp_w2_k, cmp_pe_v, cmp_w1_v, cmp_w2_v, w_nsa_out, w_out, norm2_g, w_router,
           router_bias, w_sh_gate, w_sh_up, w_sh_down, w_gate, w_up, w_down, final_g):
    B, S, D = x.shape
    T = B * S
    G, dh = N_KV_GROUPS, HEAD_DIM
    n_q = N_HEADS * dh
    n_kv = 3 * 2 * G * dh
    n_ng = 3 * N_HEADS
    mod = ada_modulation(c, w_ada, b_ada)
    sh1, sc1, g1, sh2, sc2, g2 = jnp.split(mod, 6, axis=-1)

    c_q = 2 * D_CONV
    c_kv = c_q + n_q
    c_ng = c_kv + n_kv
    c_mg = c_ng + n_ng
    pad = LANES - n_ng
    w_r = jnp.concatenate([w_in[:, :c_kv], w_in[:, c_mg:], w_in[:, c_kv:c_mg], jnp.zeros((D, pad), F32)], axis=1)
    b_r = jnp.concatenate([b_in[:c_kv], b_in[c_mg:], b_in[c_kv:c_mg], jnp.zeros((pad,), F32)])
    col_merge = c_kv
    col_kv = col_merge + 2 * D
    col_ng = col_kv + n_kv
    x2 = x.reshape(T, D)
    proj = in_proj(x2, norm1_g, sc1, sh1, w_r.astype(BF16), b_r, S)

    za = conformer_conv(proj, B, S, D, col_merge, conv_w, conv_b, conv_ln_g, conv_ln_b, w_conv_out, b_conv_out)

    kv = proj[:, col_kv:col_kv + n_kv].reshape(B, S, 3, 2, G, dh)
    NC = S // CMP_STRIDE
    zc = kv[:, :, 0].reshape(B, NC, CMP_STRIDE, 2, G, dh).transpose(0, 3, 4, 1, 2, 5).reshape(B, 2, G, NC, CMP_STRIDE * dh)
    kvc = compress_kv(zc, jnp.stack([cmp_pe_k, cmp_pe_v]), jnp.stack([cmp_w1_k, cmp_w1_v]),
                      jnp.stack([cmp_w2_k, cmp_w2_v]))

    cs = CMP_STRIDE * jnp.arange(NC)[:, None]
    ss = SEL_BLOCK * jnp.arange(LANES)[None, :]
    overlap = ((cs <= ss + SEL_BLOCK - 1) & (cs + CMP_BLOCK - 1 >= ss) & (jnp.arange(NC)[:, None] < NC - 1)
               & (jnp.arange(LANES)[None, :] < S // SEL_BLOCK)).astype(BF16)
    oc, selbias = cmp_attention(proj, kvc[:, 0].transpose(0, 1, 3, 2), kvc[:, 1], overlap, B, S, c_q)

    keys_t = lambda br: kv[:, :, br, 0].transpose(0, 2, 3, 1)
    ones_col = jnp.concatenate([jnp.ones((B, G, S, 1), BF16), jnp.zeros((B, G, S, LANES - dh - 1), BF16)], axis=-1)
    vals_aug = lambda br: jnp.concatenate([kv[:, :, br, 1].transpose(0, 2, 1, 3), ones_col], axis=-1)
    onehot_t = (jnp.arange(LANES)[:, None] == jnp.arange(S)[None, :] // SEL_BLOCK).astype(BF16)
    ksa_t = jnp.concatenate([jnp.broadcast_to(onehot_t, (B, G, LANES, S)), keys_t(1)], axis=2)
    o = nsa_attention(proj, selbias, ksa_t, vals_aug(1), keys_t(2), vals_aug(2), oc, B, S, c_q, col_ng)

    x1, h2, idx, wts = out_proj(o, za, proj, x2, g1, norm2_g, sc2, sh2, w_nsa_out, w_out, w_router, router_bias,
                                S, col_merge + D)

    plan = _dispatch_plan(idx[:, :TOP_K], T, EXPERT_BLOCK)
    y8 = routed_experts(h2, plan, w_gate, w_up, w_down, T * TOP_K)
    out = combine(y8, h2, x1, wts, g2, final_g, w_sh_gate, w_sh_up, w_sh_down, S)
    return out.reshape(B, S, D)


def kernel(x, c, w_ada, b_ada, norm1_g, w_in, b_in, conv_w, conv_b, conv_ln_g, conv_ln_b, w_conv_out, b_conv_out, cmp_pe_k, cmp_w1_k, cmp_w2_k, cmp_pe_v, cmp_w1_v, cmp_w2_v, w_nsa_out, w_out, norm2_g, w_router, router_bias, w_sh_gate, w_sh_up, w_sh_down, w_gate, w_up, w_down, final_g):
    assert w_ada.shape[0] == 1, "single-layer block"
    return _layer(x, c, w_ada[0], b_ada[0], norm1_g[0], w_in[0], b_in[0], conv_w[0], conv_b[0], conv_ln_g[0],
                  conv_ln_b[0], w_conv_out[0], b_conv_out[0], cmp_pe_k[0], cmp_w1_k[0], cmp_w2_k[0], cmp_pe_v[0],
                  cmp_w1_v[0], cmp_w2_v[0], w_nsa_out[0], w_out[0], norm2_g[0], w_router[0], router_bias[0],
                  w_sh_gate[0], w_sh_up[0], w_sh_down[0], w_gate[0], w_up[0], w_down[0], final_g)
```

```python
import functools

import jax
import jax.numpy as jnp
from jax import lax
from jax.experimental import pallas as pl
from jax.experimental.pallas import tpu as pltpu

BF16 = jnp.bfloat16
F32 = jnp.float32
I32 = jnp.int32

D_CONV = 512
CONV_K = 31
N_HEADS = 16
N_KV_GROUPS = 4
HEADS_PER_GROUP = 4
HEAD_DIM = 64
CMP_BLOCK = 32
CMP_STRIDE = 16
SEL_BLOCK = 64
SEL_TOPK = 16
WINDOW = 512
FORCED_SCORE = 1e4
N_EXPERTS = 256
TOP_K = 8
N_GROUPS = 8
TOPK_GROUPS = 4
ROUTED_SCALE = 2.5
EPS = 1e-6
NEG = -1e30
MASK_BIAS = -1e9
LANES = 128
SUBLANES = 8
HALO = 32
ATT_TILE = 256
EXPERT_BLOCK = 256
COMBINE_TILE = 256
VMEM_LIMIT = 56 * 1024 * 1024


def _params(*sem):
    return pltpu.CompilerParams(dimension_semantics=sem, vmem_limit_bytes=VMEM_LIMIT)


def _dot(a, b):
    return jnp.dot(a, b, preferred_element_type=F32)


def _split_bf16(x):
    hi = x.astype(BF16)
    lo = (x - hi.astype(F32)).astype(BF16)
    return hi, lo


def _token_chunk(ref, c, tokens):
    return ref[pl.ds(c, tokens, stride=SUBLANES), :]


def _store_token_tiles(ref, value):
    for c in range(SUBLANES):
        ref[pl.ds(c, value.shape[0], stride=SUBLANES), :] = value[:, c * LANES:(c + 1) * LANES]


def _ada_kernel(c_ref, w_ref, b_ref, o_ref):
    c = c_ref[...]
    a = c * jax.nn.sigmoid(c)
    o_ref[...] = _dot(a.astype(BF16), w_ref[...].astype(BF16)) + b_ref[...]


def ada_modulation(c, w_ada, b_ada):
    B, D = c.shape
    N = w_ada.shape[1]
    rows = 8
    c8 = jnp.zeros((rows, D), F32).at[:B].set(c)
    tn = 1024
    out = pl.pallas_call(
        _ada_kernel,
        grid=(N // tn,),
        in_specs=[pl.BlockSpec((rows, D), lambda j: (0, 0)),
                  pl.BlockSpec((D, tn), lambda j: (0, j)),
                  pl.BlockSpec((1, tn), lambda j: (0, j))],
        out_specs=pl.BlockSpec((rows, tn), lambda j: (0, j)),
        out_shape=jax.ShapeDtypeStruct((rows, N), F32),
        compiler_params=_params("arbitrary"),
        name="ada",
    )(c8, w_ada, b_ada.reshape(1, N))
    return out[:B]


def _modulated_rmsnorm(x, g, sc, sh):
    y = x * lax.rsqrt(jnp.mean(x * x, axis=-1, keepdims=True) + EPS)
    return (y * g) * (1.0 + sc) + sh


def _in_proj_kernel(x_ref, g_ref, sc_ref, sh_ref, w_ref, b_ref, o_ref, h_scr):
    @pl.when(pl.program_id(1) == 0)
    def _():
        h = _modulated_rmsnorm(x_ref[...], g_ref[...], sc_ref[...], sh_ref[...])
        h_scr[...] = h.astype(BF16)

    o_ref[...] = (_dot(h_scr[...], w_ref[...]) + b_ref[...]).astype(BF16)


def in_proj(x2, norm_g, sc, sh, w, b, seq, tm=1024, tn=1152):
    T, D = x2.shape
    NP = w.shape[1]
    per_b = seq // tm
    return pl.pallas_call(
        _in_proj_kernel,
        grid=(T // tm, NP // tn),
        in_specs=[pl.BlockSpec((tm, D), lambda i, j: (i, 0)),
                  pl.BlockSpec((1, D), lambda i, j: (0, 0)),
                  pl.BlockSpec((None, 1, D), lambda i, j: (i // per_b, 0, 0)),
                  pl.BlockSpec((None, 1, D), lambda i, j: (i // per_b, 0, 0)),
                  pl.BlockSpec((D, tn), lambda i, j: (0, j)),
                  pl.BlockSpec((1, tn), lambda i, j: (0, j))],
        out_specs=pl.BlockSpec((tm, tn), lambda i, j: (i, j)),
        out_shape=jax.ShapeDtypeStruct((T, NP), BF16),
        scratch_shapes=[pltpu.VMEM((tm, D), BF16)],
        compiler_params=_params("arbitrary", "arbitrary"),
        name="in_proj",
    )(x2, norm_g.reshape(1, D), sc[:, None, :], sh[:, None, :], w, b.reshape(1, NP))


def _conv_kernel(a_ref, gt_ref, pa_ref, pg_ref, mg_ref, cw_ref, cb_ref, lg_ref, lb_ref, wo_ref, bo_ref,
                 o_ref, u_scr, c_scr, *, ts, chunk):
    i = pl.program_id(1)
    a = a_ref[...].astype(F32)
    u_scr[HALO:, :] = a * jax.nn.sigmoid(gt_ref[...].astype(F32))
    pa = pa_ref[...].astype(F32)
    prev = pa * jax.nn.sigmoid(pg_ref[...].astype(F32))
    u_scr[:HALO, :] = jnp.where(i > 0, prev, 0.0)
    off = HALO - (CONV_K - 1)
    for r0 in range(0, ts, chunk):
        acc = jnp.broadcast_to(cb_ref[...], (chunk, D_CONV))
        for k in range(CONV_K):
            acc = acc + cw_ref[k:k + 1, :] * u_scr[r0 + off + k:r0 + off + k + chunk, :]
        c_scr[r0:r0 + chunk, :] = acc
    v = c_scr[...]
    mu = jnp.mean(v, axis=-1, keepdims=True)
    var = jnp.mean(jnp.square(v - mu), axis=-1, keepdims=True)
    y = (v - mu) * lax.rsqrt(var + EPS) * lg_ref[...] + lb_ref[...]
    y = y * jax.nn.sigmoid(y)
    ya = _dot(y.astype(BF16), wo_ref[...]) + bo_ref[...]
    o_ref[...] = (jax.nn.sigmoid(mg_ref[...].astype(F32)) * ya).astype(BF16)


def conformer_conv(proj, B, S, D, merge_col, conv_w, conv_b, ln_g, ln_b, w_o, b_o, ts=512, chunk=64):
    T = B * S
    nt = S // ts
    hb = ts // HALO
    prev_idx = lambda b, i: (jnp.maximum((b * nt + i) * hb - 1, 0), 0)
    prev_idx_g = lambda b, i: (jnp.maximum((b * nt + i) * hb - 1, 0), 1)
    row = lambda v: v.reshape(1, -1)
    return pl.pallas_call(
        functools.partial(_conv_kernel, ts=ts, chunk=chunk),
        grid=(B, nt),
        in_specs=[pl.BlockSpec((ts, D_CONV), lambda b, i: (b * nt + i, 0)),
                  pl.BlockSpec((ts, D_CONV), lambda b, i: (b * nt + i, 1)),
                  pl.BlockSpec((HALO, D_CONV), prev_idx),
                  pl.BlockSpec((HALO, D_CONV), prev_idx_g),
                  pl.BlockSpec((ts, D), lambda b, i: (b * nt + i, merge_col // D)),
                  pl.BlockSpec((CONV_K, D_CONV), lambda b, i: (0, 0)),
                  pl.BlockSpec((1, D_CONV), lambda b, i: (0, 0)),
                  pl.BlockSpec((1, D_CONV), lambda b, i: (0, 0)),
                  pl.BlockSpec((1, D_CONV), lambda b, i: (0, 0)),
                  pl.BlockSpec((D_CONV, D), lambda b, i: (0, 0)),
                  pl.BlockSpec((1, D), lambda b, i: (0, 0))],
        out_specs=pl.BlockSpec((ts, D), lambda b, i: (b * nt + i, 0)),
        out_shape=jax.ShapeDtypeStruct((T, D), BF16),
        scratch_shapes=[pltpu.VMEM((ts + HALO, D_CONV), F32), pltpu.VMEM((ts, D_CONV), F32)],
        compiler_params=_params("arbitrary", "arbitrary"),
        name="conv",
    )(proj, proj, proj, proj, proj, conv_w, row(conv_b), row(ln_g), row(ln_b), w_o.astype(BF16), row(b_o))


def _compress_kernel(z_ref, pe_ref, w1_ref, w2_ref, o_ref, *, nc):
    half = w1_ref.shape[0] // 2
    z = z_ref[...]
    first = _dot(z, w1_ref[:half, :])
    second = _dot(z, w1_ref[half:, :])
    bias = _dot(pe_ref[...], w1_ref[...])[0:1, :]
    hid = first + pltpu.roll(second, nc - 1, 0) + bias
    act = jax.nn.gelu(hid)
    out = _dot(act.astype(BF16), w2_ref[...])
    rows = lax.broadcasted_iota(I32, out.shape, 0)
    o_ref[...] = jnp.where(rows < nc - 1, out, 0.0).astype(BF16)


def compress_kv(zc, pe, w1, w2):
    B, _, G, NC, W = zc.shape
    H = w1.shape[-1]
    pe8 = jnp.broadcast_to(pe.reshape(2, 1, 2 * W), (2, 8, 2 * W)).astype(BF16)
    return pl.pallas_call(
        functools.partial(_compress_kernel, nc=NC),
        grid=(B, 2, G),
        in_specs=[pl.BlockSpec((None, None, None, NC, W), lambda b, s, g: (b, s, g, 0, 0)),
                  pl.BlockSpec((None, 8, 2 * W), lambda b, s, g: (s, 0, 0)),
                  pl.BlockSpec((None, 2 * W, H), lambda b, s, g: (s, 0, 0)),
                  pl.BlockSpec((None, H, HEAD_DIM), lambda b, s, g: (s, 0, 0))],
        out_specs=pl.BlockSpec((None, None, None, NC, HEAD_DIM), lambda b, s, g: (b, s, g, 0, 0)),
        out_shape=jax.ShapeDtypeStruct((B, 2, G, NC, HEAD_DIM), BF16),
        compiler_params=_params("arbitrary", "arbitrary", "arbitrary"),
        name="compress",
    )(zc, pe8, w1.astype(BF16), w2.astype(BF16))


def _cmp_attn_kernel(q_ref, kct_ref, vc_ref, ov_ref, oc_ref, sb_ref, qa_scr, imp_scr, *, tq, ns):
    i = pl.program_id(2)
    t0 = i * tq
    HG = HEADS_PER_GROUP
    R = HG * tq
    nc = kct_ref.shape[1]
    q = q_ref[...] * (HEAD_DIM ** -0.5)
    for h in range(HG):
        qa_scr[h * tq:(h + 1) * tq, :] = q[:, h * HEAD_DIM:(h + 1) * HEAD_DIM]

    def attend(width):
        t_ids = t0 + (lax.broadcasted_iota(I32, (R, width), 0) & (tq - 1))
        n_ids = lax.broadcasted_iota(I32, (R, width), 1)
        valid = (CMP_STRIDE * n_ids + (CMP_BLOCK - 1)) <= t_ids
        s = jnp.where(valid, _dot(qa_scr[...], kct_ref[:, :width]), NEG)
        m = jnp.max(s, axis=-1, keepdims=True)
        e = jnp.where(valid, jnp.exp(s - m), 0.0)
        den = jnp.sum(e, axis=-1, keepdims=True)
        p = e * jnp.where(den > 0.0, 1.0 / den, 0.0)
        o = _dot(p.astype(BF16), vc_ref[:width, :])
        for h in range(HG):
            oc_ref[:, h * HEAD_DIM:(h + 1) * HEAD_DIM] = o[h * tq:(h + 1) * tq, :].astype(BF16)
        psum = p[:tq]
        for h in range(1, HG):
            psum = psum + p[h * tq:(h + 1) * tq]
        ps_hi, ps_lo = _split_bf16(psum)
        imp_scr[...] = _dot(ps_hi, ov_ref[:width, :]) + _dot(ps_lo, ov_ref[:width, :])

    visible = (t0 + tq - (CMP_BLOCK - CMP_STRIDE)) // CMP_STRIDE
    n_tiles = jnp.clip((visible + LANES - 1) // LANES, 1, nc // LANES)
    for k in range(1, nc // LANES + 1):
        pl.when(n_tiles == k)(functools.partial(attend, k * LANES))

    imp = imp_scr[...]
    lane = lax.broadcasted_iota(I32, (tq, LANES), 1)
    jq = (t0 + lax.broadcasted_iota(I32, (tq, LANES), 0)) >> (SEL_BLOCK.bit_length() - 1)
    forced = (lane == 0) | (lane == jq) | (lane == jq - 1)
    imp = jnp.where(forced, FORCED_SCORE, jnp.where(lane <= jq, imp, -1.0))
    work = jnp.where(lane < ns, imp, -jnp.inf).T
    blk_f = lax.broadcasted_iota(I32, (LANES, tq), 0).astype(F32)
    sel = jnp.zeros((LANES, tq), F32)
    for _ in range(SEL_TOPK):
        top = jnp.max(work, axis=0, keepdims=True)
        first = jnp.min(jnp.where(work == top, blk_f, float(LANES)), axis=0, keepdims=True)
        pick = blk_f == first
        sel = jnp.where(pick, 1.0, sel)
        work = jnp.where(pick, -jnp.inf, work)
    sb_ref[...] = jnp.where((sel.T > 0.5) & (lane <= jq), 0.0, MASK_BIAS).astype(BF16)


def cmp_attention(proj, kc_t, vc, overlap, B, S, q_col, tq=ATT_TILE):
    G = N_KV_GROUPS
    NC = vc.shape[2]
    nt = S // tq
    gw = HEADS_PER_GROUP * HEAD_DIM
    ns = S // SEL_BLOCK
    assert SEL_TOPK <= ns <= LANES and tq & (tq - 1) == 0 and NC % LANES == 0
    return pl.pallas_call(
        functools.partial(_cmp_attn_kernel, tq=tq, ns=ns),
        grid=(B, G, nt),
        in_specs=[pl.BlockSpec((tq, gw), lambda b, g, i: (b * nt + i, q_col // gw + g)),
                  pl.BlockSpec((None, None, HEAD_DIM, NC), lambda b, g, i: (b, g, 0, 0)),
                  pl.BlockSpec((None, None, NC, HEAD_DIM), lambda b, g, i: (b, g, 0, 0)),
                  pl.BlockSpec((NC, LANES), lambda b, g, i: (0, 0))],
        out_specs=[pl.BlockSpec((tq, gw), lambda b, g, i: (b * nt + i, g)),
                   pl.BlockSpec((None, None, tq, LANES), lambda b, g, i: (b, g, i, 0))],
        out_shape=[jax.ShapeDtypeStruct((B * S, G * gw), BF16),
                   jax.ShapeDtypeStruct((B, G, S, LANES), BF16)],
        scratch_shapes=[pltpu.VMEM((HEADS_PER_GROUP * tq, HEAD_DIM), BF16), pltpu.VMEM((tq, LANES), F32)],
        compiler_params=_params("arbitrary", "arbitrary", "arbitrary"),
        name="cmp_attn",
    )(proj, kc_t, vc, overlap)


def _nsa_kernel(q_ref, sb_ref, ksa_ref, vs_ref, kw_ref, vw_ref, oc_ref, ng_ref, o_ref,
                qa_scr, ms_scr, as_scr, mw_scr, aw_scr, *, tq, unroll):
    g = pl.program_id(1)
    i = pl.program_id(2)
    HG = HEADS_PER_GROUP
    R = HG * tq
    q = q_ref[...] * (HEAD_DIM ** -0.5)
    sb = sb_ref[...]
    for h in range(HG):
        qa_scr[h * tq:(h + 1) * tq, :LANES] = sb
        qa_scr[h * tq:(h + 1) * tq, LANES:] = q[:, h * HEAD_DIM:(h + 1) * HEAD_DIM]

    q_loc = lax.broadcasted_iota(I32, (R, tq), 0) & (tq - 1)
    k_loc = lax.broadcasted_iota(I32, (R, tq), 1)
    causal = lambda: k_loc <= q_loc
    strictly_upper = lambda: k_loc > q_loc

    def step(kt, mask, k_ref, v_ref, q_lo, m_scr, acc_scr):
        start = pl.multiple_of(kt * tq, tq)
        s = _dot(qa_scr[:, q_lo:], k_ref[:, pl.ds(start, tq)])
        if mask is not None:
            s = jnp.where(mask(), s, NEG)
        m_prev = m_scr[...]
        m_new = jnp.maximum(m_prev, jnp.max(s, axis=-1, keepdims=True))
        alpha = jnp.exp(m_prev - m_new)
        p = jnp.concatenate([jnp.exp(s[:, j * LANES:(j + 1) * LANES] - m_new) for j in range(tq // LANES)], axis=-1)
        acc_scr[...] = alpha * acc_scr[...] + _dot(p.astype(BF16), v_ref[pl.ds(start, tq), :])
        m_scr[...] = m_new

    sel = functools.partial(step, k_ref=ksa_ref, v_ref=vs_ref, q_lo=0, m_scr=ms_scr, acc_scr=as_scr)
    win = functools.partial(step, k_ref=kw_ref, v_ref=vw_ref, q_lo=LANES, m_scr=mw_scr, acc_scr=aw_scr)

    for m_scr, acc_scr in ((ms_scr, as_scr), (mw_scr, aw_scr)):
        m_scr[...] = jnp.full(m_scr.shape, NEG, F32)
        acc_scr[...] = jnp.zeros(acc_scr.shape, F32)

    def sel_unrolled(kp, carry):
        for u in range(unroll):
            sel(unroll * kp + u, None)
        return carry

    def sel_single(kt, carry):
        sel(kt, None)
        return carry

    lax.fori_loop(0, i // unroll, sel_unrolled, 0)
    lax.fori_loop((i // unroll) * unroll, i, sel_single, 0)

    @pl.when(i >= 2)
    def _():
        win(i - 2, strictly_upper)
        sel(i, causal)
        win(i - 1, None)
        win(i, causal)

    @pl.when(i < 2)
    def _():
        sel(i, causal)

        @pl.when(i == 1)
        def _():
            win(0, None)

        win(i, causal)

    acc_s = as_scr[...]
    o_sel = acc_s[:, :HEAD_DIM] / acc_s[:, HEAD_DIM:HEAD_DIM + 1]
    acc_w = aw_scr[...]
    o_win = acc_w[:, :HEAD_DIM] / acc_w[:, HEAD_DIM:HEAD_DIM + 1]

    gates = jax.nn.sigmoid(ng_ref[...].astype(F32))
    oc = oc_ref[...].astype(F32)
    lane = lax.broadcasted_iota(I32, gates.shape, 1)
    for h in range(HG):
        c0 = (g * HG + h) * 3
        gc = jnp.sum(jnp.where(lane == c0, gates, 0.0), axis=-1, keepdims=True)
        gs = jnp.sum(jnp.where(lane == c0 + 1, gates, 0.0), axis=-1, keepdims=True)
        gw = jnp.sum(jnp.where(lane == c0 + 2, gates, 0.0), axis=-1, keepdims=True)
        rows = slice(h * tq, (h + 1) * tq)
        cols = slice(h * HEAD_DIM, (h + 1) * HEAD_DIM)
        o_ref[:, cols] = (gc * oc[:, cols] + gs * o_sel[rows, :] + gw * o_win[rows, :]).astype(BF16)


def nsa_attention(proj, selbias, ksa_t, vs_aug, kw_t, vw_aug, oc, B, S, q_col, ng_col, tq=ATT_TILE, unroll=4):
    G = N_KV_GROUPS
    nt = S // tq
    gw = HEADS_PER_GROUP * HEAD_DIM
    assert WINDOW == 2 * tq and tq & (tq - 1) == 0
    R = HEADS_PER_GROUP * tq
    per_group = lambda rows, cols: pl.BlockSpec((None, None, rows, cols), lambda b, g, i: (b, g, 0, 0))
    return pl.pallas_call(
        functools.partial(_nsa_kernel, tq=tq, unroll=unroll),
        grid=(B, G, nt),
        in_specs=[pl.BlockSpec((tq, gw), lambda b, g, i: (b * nt + i, q_col // gw + g)),
                  pl.BlockSpec((None, None, tq, LANES), lambda b, g, i: (b, g, i, 0)),
                  per_group(LANES + HEAD_DIM, S), per_group(S, LANES), per_group(HEAD_DIM, S), per_group(S, LANES),
                  pl.BlockSpec((tq, gw), lambda b, g, i: (b * nt + i, g)),
                  pl.BlockSpec((tq, LANES), lambda b, g, i: (b * nt + i, ng_col // LANES))],
        out_specs=pl.BlockSpec((tq, gw), lambda b, g, i: (b * nt + i, g)),
        out_shape=jax.ShapeDtypeStruct((B * S, G * gw), BF16),
        scratch_shapes=[pltpu.VMEM((R, LANES + HEAD_DIM), BF16),
                        pltpu.VMEM((R, LANES), F32), pltpu.VMEM((R, LANES), F32),
                        pltpu.VMEM((R, LANES), F32), pltpu.VMEM((R, LANES), F32)],
        compiler_params=_params("arbitrary", "arbitrary", "arbitrary"),
        name="nsa_attn",
    )(proj, selbias, ksa_t, vs_aug, kw_t, vw_aug, oc, proj)


def _route(logits, rbias_col):
    tm, E = logits.shape
    gsz = E // N_GROUPS
    lt = logits.T
    scores = jax.nn.sigmoid(lt)
    sfc = scores + rbias_col
    row_f = lax.broadcasted_iota(I32, (E, tm), 0).astype(F32)
    gscore = []
    for g in range(N_GROUPS):
        v = sfc[g * gsz:(g + 1) * gsz]
        m1 = jnp.max(v, axis=0, keepdims=True)
        cnt = jnp.sum(jnp.where(v == m1, 1.0, 0.0), axis=0, keepdims=True)
        m2 = jnp.max(jnp.where(v < m1, v, -jnp.inf), axis=0, keepdims=True)
        gscore.append(m1 + jnp.where(cnt >= 2.0, m1, m2))
    masked = []
    for g in range(N_GROUPS):
        rank = jnp.zeros((1, tm), F32)
        for o in range(N_GROUPS):
            if o == g:
                continue
            ahead = (gscore[o] > gscore[g]) | ((gscore[o] == gscore[g]) & (o < g))
            rank = rank + jnp.where(ahead, 1.0, 0.0)
        masked.append(jnp.where(rank < float(TOPK_GROUPS), sfc[g * gsz:(g + 1) * gsz], NEG))
    work = jnp.concatenate(masked, axis=0)
    out_row = lax.broadcasted_iota(I32, (LANES, tm), 0)
    idx_out = jnp.zeros((LANES, tm), F32)
    w_out = jnp.zeros((LANES, tm), F32)
    wsum = jnp.zeros((1, tm), F32)
    for r in range(TOP_K):
        top = jnp.max(work, axis=0, keepdims=True)
        first = jnp.min(jnp.where(work == top, row_f, float(E)), axis=0, keepdims=True)
        pick = row_f == first
        w = jnp.sum(jnp.where(pick, scores, 0.0), axis=0, keepdims=True)
        idx_out = jnp.where(out_row == r, first, idx_out)
        w_out = jnp.where(out_row == r, w, w_out)
        wsum = wsum + w
        work = jnp.where(pick, -jnp.inf, work)
    return idx_out.T.astype(I32), (w_out / wsum * ROUTED_SCALE).T


def _out_proj_kernel(o_ref, za_ref, mb_ref, x_ref, g1_ref, n2_ref, sc_ref, sh_ref, wn_ref, wo_ref,
                     wrh_ref, wrl_ref, rb_ref, x1_ref, h2_ref, idx_ref, wt_ref):
    yb = _dot(o_ref[...], wn_ref[...])
    z = za_ref[...].astype(F32) + jax.nn.sigmoid(mb_ref[...].astype(F32)) * yb
    mix = _dot(z.astype(BF16), wo_ref[...])
    x1 = x_ref[...] + g1_ref[...] * mix
    x1_ref[...] = x1
    h2 = _modulated_rmsnorm(x1, n2_ref[...], sc_ref[...], sh_ref[...])
    _store_token_tiles(h2_ref, h2)
    hi, lo = _split_bf16(h2)
    logits = _dot(hi, wrh_ref[...]) + (_dot(lo, wrh_ref[...]) + _dot(hi, wrl_ref[...]))
    idx, wts = _route(logits, rb_ref[...])
    idx_ref[...] = idx
    wt_ref[...] = wts


def out_proj(o, za, proj, x2, g1, norm_g, sc, sh, w_nsa_out, w_out, w_router, router_bias, seq, mb_col, tm=256):
    T, D = x2.shape
    assert D == SUBLANES * LANES
    E = w_router.shape[1]
    per_b = seq // tm
    wr_hi, wr_lo = _split_bf16(w_router)
    tok = lambda i: (i, 0)
    const = lambda i: (0, 0)
    per_batch = pl.BlockSpec((None, 1, D), lambda i: (i // per_b, 0, 0))
    return pl.pallas_call(
        _out_proj_kernel,
        grid=(T // tm,),
        in_specs=[pl.BlockSpec((tm, D), tok), pl.BlockSpec((tm, D), tok),
                  pl.BlockSpec((tm, D), lambda i: (i, mb_col // D)),
                  pl.BlockSpec((tm, D), tok), per_batch,
                  pl.BlockSpec((1, D), const), per_batch, per_batch,
                  pl.BlockSpec((D, D), const), pl.BlockSpec((D, D), const),
                  pl.BlockSpec((D, E), const), pl.BlockSpec((D, E), const), pl.BlockSpec((E, 1), const)],
        out_specs=[pl.BlockSpec((tm, D), tok), pl.BlockSpec((tm * SUBLANES, LANES), tok),
                   pl.BlockSpec((tm, LANES), tok), pl.BlockSpec((tm, LANES), tok)],
        out_shape=[jax.ShapeDtypeStruct((T, D), F32), jax.ShapeDtypeStruct((T * SUBLANES, LANES), F32),
                   jax.ShapeDtypeStruct((T, LANES), I32), jax.ShapeDtypeStruct((T, LANES), F32)],
        compiler_params=_params("arbitrary"),
        name="out_proj",
    )(o, za, proj, x2, g1[:, None, :], norm_g.reshape(1, D), sc[:, None, :], sh[:, None, :],
      w_nsa_out.astype(BF16), w_out.astype(BF16), wr_hi, wr_lo, router_bias.reshape(E, 1))


def _experts_kernel(ex_ref, ni_ref, gidx_hbm, sidx_hbm, h_hbm, wg0, wu0, wd0, wg1, wu1, wd1, y_hbm,
                    g_idx, s_idx, x_buf, y_buf, g_sem, s_sem, row_sem, out_sem, *, bm, dump_row):
    n = pl.program_id(0)
    n_items = ni_ref[0]
    dump_idx_row = ni_ref[1]
    w0 = 2 * n
    w1 = w0 + 1

    def idx_copy(src_hbm, w, smem, sem, s):
        return pltpu.make_async_copy(src_hbm.at[w], smem.at[s], sem.at[s])

    gather_idx = functools.partial(idx_copy, gidx_hbm, smem=g_idx, sem=g_sem)
    scatter_idx = functools.partial(idx_copy, sidx_hbm, smem=s_idx, sem=s_sem)

    def gather_rows(s, rows):
        for r in rows:
            src = pl.multiple_of(g_idx[s, r] * SUBLANES, SUBLANES)
            pltpu.make_async_copy(h_hbm.at[pl.ds(src, SUBLANES), :], x_buf.at[s, pl.ds(r * SUBLANES, SUBLANES), :],
                                  row_sem.at[s]).start(priority=r % 2)

    def scatter_rows(s, rows):
        for r in rows:
            dst = pl.multiple_of(s_idx[s, r] * SUBLANES, SUBLANES)
            pltpu.make_async_copy(y_buf.at[s, pl.ds(r * SUBLANES, SUBLANES), :], y_hbm.at[pl.ds(dst, SUBLANES), :],
                                  out_sem.at[s]).start(priority=r % 2)

    def wait_gather(s):
        pltpu.make_async_copy(h_hbm.at[pl.ds(0, bm * SUBLANES), :], x_buf.at[s], row_sem.at[s]).wait()

    def wait_scatter(s):
        pltpu.make_async_copy(y_buf.at[s], y_hbm.at[pl.ds(0, bm * SUBLANES), :], out_sem.at[s]).wait()

    phases = 2 * SUBLANES
    per_phase = bm // phases

    def expert(s, wg, wu, wd):
        other = 1 - s

        def other_slot_dmas(ph):
            rows = range(ph * per_phase, (ph + 1) * per_phase)
            gather_rows(other, rows)
            scatter_rows(other, rows)

        gate = up = None
        for c in range(SUBLANES):
            xc = _token_chunk(x_buf.at[s], c, bm).astype(BF16)
            g = _dot(xc, wg[c * LANES:(c + 1) * LANES, :].astype(BF16))
            u = _dot(xc, wu[c * LANES:(c + 1) * LANES, :].astype(BF16))
            gate = g if gate is None else gate + g
            up = u if up is None else up + u
            other_slot_dmas(c)
        hid = (gate * jax.nn.sigmoid(gate) * up).astype(BF16)
        for c in range(SUBLANES):
            y_buf[s, pl.ds(c, bm, stride=SUBLANES), :] = _dot(hid, wd[:, c * LANES:(c + 1) * LANES].astype(BF16))
            other_slot_dmas(SUBLANES + c)

    @pl.when(n == 0)
    def _():
        y_buf[...] = jnp.zeros(y_buf.shape, F32)
        pltpu.make_async_copy(y_buf.at[0], y_hbm.at[pl.ds(dump_row * SUBLANES, bm * SUBLANES), :], out_sem.at[0]).start()
        gather_idx(w=0, s=0).start()
        gather_idx(w=1, s=1).start()
        scatter_idx(w=dump_idx_row, s=1).start()
        gather_idx(w=0, s=0).wait()
        gather_rows(0, range(bm))
        scatter_idx(w=dump_idx_row, s=1).wait()

    @pl.when(w0 < n_items)
    def _():
        gather_idx(w=w0 + 2, s=0).start()
        scatter_idx(w=w0, s=0).start()
        gather_idx(w=w1, s=1).wait()
        wait_gather(0)
        wait_scatter(0)
        expert(0, wg0, wu0, wd0)
        scatter_idx(w=w1, s=1).start()
        gather_idx(w=w1 + 2, s=1).start()
        scatter_idx(w=w0, s=0).wait()
        gather_idx(w=w0 + 2, s=0).wait()
        wait_gather(1)
        wait_scatter(1)
        expert(1, wg1, wu1, wd1)
        scatter_idx(w=w1, s=1).wait()

        @pl.when(w0 + 2 >= n_items)
        def _():
            scatter_rows(1, range(bm))
            wait_gather(0)
            gather_idx(w=w1 + 2, s=1).wait()
            wait_scatter(0)
            wait_scatter(1)


def routed_experts(h2, plan, w_gate, w_up, w_down, n_assign, bm=EXPERT_BLOCK):
    ex, n_items, gidx, sidx = plan
    E, D, DE = w_gate.shape
    assert D == SUBLANES * LANES and bm % (2 * SUBLANES) == 0
    n_work = ex.shape[0] - 2
    assert n_work % 2 == 0 and sidx.shape[0] == n_work + 3
    tile_rows = bm * SUBLANES
    even = lambda n, ex, ni: (ex[2 * n], 0, 0)
    odd = lambda n, ex, ni: (ex[2 * n + 1], 0, 0)
    grid_spec = pltpu.PrefetchScalarGridSpec(
        num_scalar_prefetch=2,
        grid=(n_work // 2,),
        in_specs=[pl.BlockSpec(memory_space=pl.ANY), pl.BlockSpec(memory_space=pl.ANY),
                  pl.BlockSpec(memory_space=pl.ANY),
                  pl.BlockSpec((None, D, DE), even), pl.BlockSpec((None, D, DE), even), pl.BlockSpec((None, DE, D), even),
                  pl.BlockSpec((None, D, DE), odd), pl.BlockSpec((None, D, DE), odd), pl.BlockSpec((None, DE, D), odd)],
        out_specs=pl.BlockSpec(memory_space=pl.ANY),
        scratch_shapes=[pltpu.SMEM((2, bm), I32), pltpu.SMEM((2, bm), I32),
                        pltpu.VMEM((2, tile_rows, LANES), F32), pltpu.VMEM((2, tile_rows, LANES), F32),
                        pltpu.SemaphoreType.DMA((2,)), pltpu.SemaphoreType.DMA((2,)),
                        pltpu.SemaphoreType.DMA((2,)), pltpu.SemaphoreType.DMA((2,))],
    )
    return pl.pallas_call(
        functools.partial(_experts_kernel, bm=bm, dump_row=n_assign),
        grid_spec=grid_spec,
        out_shape=jax.ShapeDtypeStruct(((n_assign + 2 * bm) * SUBLANES, LANES), F32),
        compiler_params=_params("arbitrary"),
        name="experts",
    )(ex, n_items, gidx, sidx, h2, w_gate, w_up, w_down, w_gate, w_up, w_down)


def _combine_kernel(*refs):
    y8_refs = refs[:TOP_K]
    h2_ref, x1_ref, wt_ref, g2_ref, fg_ref, wsg_ref, wsu_ref, wsd_ref, o_ref = refs[TOP_K:]
    tj = x1_ref.shape[0]
    gate = up = None
    for c in range(SUBLANES):
        hc = _token_chunk(h2_ref, c, tj).astype(BF16)
        g = _dot(hc, wsg_ref[c * LANES:(c + 1) * LANES, :])
        u = _dot(hc, wsu_ref[c * LANES:(c + 1) * LANES, :])
        gate = g if gate is None else gate + g
        up = u if up is None else up + u
    hid = (gate * jax.nn.sigmoid(gate) * up).astype(BF16)
    wt = wt_ref[...]
    chunks = []
    for c in range(SUBLANES):
        yc = _dot(hid, wsd_ref[:, c * LANES:(c + 1) * LANES])
        for k in range(TOP_K):
            yc = yc + wt[:, k:k + 1] * _token_chunk(y8_refs[k], c, tj)
        chunks.append(yc)
    x2 = x1_ref[...] + g2_ref[...] * jnp.concatenate(chunks, axis=-1)
    o_ref[...] = x2 * lax.rsqrt(jnp.mean(x2 * x2, axis=-1, keepdims=True) + EPS) * fg_ref[...]


def combine(y8, h2, x1, wts, g2, final_g, w_sh_gate, w_sh_up, w_sh_down, seq, tj=COMBINE_TILE):
    T, D = x1.shape
    DS = w_sh_gate.shape[1]
    per_b = seq // tj
    nt = T // tj
    tok = lambda i: (i, 0)
    const = lambda i: (0, 0)
    slot = lambda k: pl.BlockSpec((tj * SUBLANES, LANES), lambda i: (k * nt + i, 0))
    return pl.pallas_call(
        _combine_kernel,
        grid=(nt,),
        in_specs=[slot(k) for k in range(TOP_K)] + [
                  pl.BlockSpec((tj * SUBLANES, LANES), tok), pl.BlockSpec((tj, D), tok), pl.BlockSpec((tj, LANES), tok),
                  pl.BlockSpec((None, 1, D), lambda i: (i // per_b, 0, 0)),
                  pl.BlockSpec((1, D), const),
                  pl.BlockSpec((D, DS), const), pl.BlockSpec((D, DS), const), pl.BlockSpec((DS, D), const)],
        out_specs=pl.BlockSpec((tj, D), tok),
        out_shape=jax.ShapeDtypeStruct((T, D), F32),
        compiler_params=_params("arbitrary"),
        name="combine",
    )(*([y8] * TOP_K), h2, x1, wts, g2[:, None, :], final_g.reshape(1, D),
      w_sh_gate.astype(BF16), w_sh_up.astype(BF16), w_sh_down.astype(BF16))


def _dispatch_plan(idx, T, bm):
    A = T * TOP_K
    assert A % bm == 0
    nblk = A // bm
    E = N_EXPERTS
    kbits = TOP_K.bit_length() - 1
    sorted_e, order = lax.sort((idx.reshape(A), jnp.arange(A, dtype=I32)), num_keys=1)
    bounds = jnp.searchsorted(sorted_e, jnp.arange(E + 1, dtype=I32), side='left').astype(I32)
    start, end = bounds[:-1], bounds[1:]
    first_blk = start // bm
    n_it = jnp.where(end > start, (end - 1) // bm - first_blk + 1, 0)
    it_end = jnp.cumsum(n_it)
    it_start = it_end - n_it
    n_items = it_end[-1]
    n_work = nblk + E
    w = jnp.arange(n_work + 2, dtype=I32)
    real = w < n_items
    wc = jnp.minimum(w, n_items - 1)
    ex = jnp.searchsorted(it_end, wc, side='right').astype(I32)
    blk = first_blk[ex] + wc - it_start[ex]
    lo = jnp.maximum(start[ex], blk * bm) - blk * bm
    hi = jnp.where(real, jnp.minimum(end[ex], (blk + 1) * bm) - blk * bm, lo)
    rows = order.reshape(nblk, bm)[blk]
    r = jnp.arange(bm, dtype=I32)[None, :]
    mine = (r >= lo[:, None]) & (r < hi[:, None])
    gidx = rows >> kbits
    sidx = jnp.where(mine, (rows & (TOP_K - 1)) * T + gidx, A + (w[:, None] % 2) * bm + r)
    sidx = jnp.concatenate([sidx, A + bm + r], axis=0)
    counts = jnp.stack([n_items.astype(I32), jnp.asarray(n_work + 2, I32)])
    return ex, counts, gidx.astype(I32), sidx.astype(I32)


def _layer(x, c, w_ada, b_ada, norm1_g, w_in, b_in, conv_w, conv_b, conv_ln_g, conv_ln_b, w_conv_out, b_conv_out,
           cmp_pe_k, cmp_w1_k, cmp_w2_k, cmp_pe_v, cmp_w1_v, cmp_w2_v, w_nsa_out, w_out, norm2_g, w_router,
           router_bias, w_sh_gate, w_sh_up, w_sh_down, w_gate, w_up, w_down, final_g):
    B, S, D = x.shape
    T = B * S
    G, dh = N_KV_GROUPS, HEAD_DIM
    n_q = N_HEADS * dh
    n_kv = 3 * 2 * G * dh
    n_ng = 3 * N_HEADS
    mod = ada_modulation(c, w_ada, b_ada)
    sh1, sc1, g1, sh2, sc2, g2 = jnp.split(mod, 6, axis=-1)

    c_q = 2 * D_CONV
    c_kv = c_q + n_q
    c_ng = c_kv + n_kv
    c_mg = c_ng + n_ng
    pad = LANES - n_ng
    w_r = jnp.concatenate([w_in[:, :c_kv], w_in[:, c_mg:], w_in[:, c_kv:c_mg], jnp.zeros((D, pad), F32)], axis=1)
    b_r = jnp.concatenate([b_in[:c_kv], b_in[c_mg:], b_in[c_kv:c_mg], jnp.zeros((pad,), F32)])
    col_merge = c_kv
    col_kv = col_merge + 2 * D
    col_ng = col_kv + n_kv
    x2 = x.reshape(T, D)
    proj = in_proj(x2, norm1_g, sc1, sh1, w_r.astype(BF16), b_r, S)

    za = conformer_conv(proj, B, S, D, col_merge, conv_w, conv_b, conv_ln_g, conv_ln_b, w_conv_out, b_conv_out)

    kv = proj[:, col_kv:col_kv + n_kv].reshape(B, S, 3, 2, G, dh)
    NC = S // CMP_STRIDE
    zc = kv[:, :, 0].reshape(B, NC, CMP_STRIDE, 2, G, dh).transpose(0, 3, 4, 1, 2, 5).reshape(B, 2, G, NC, CMP_STRIDE * dh)
    kvc = compress_kv(zc, jnp.stack([cmp_pe_k, cmp_pe_v]), jnp.stack([cmp_w1_k, cmp_w1_v]),
                      jnp.stack([cmp_w2_k, cmp_w2_v]))

    cs = CMP_STRIDE * jnp.arange(NC)[:, None]
    ss = SEL_BLOCK * jnp.arange(LANES)[None, :]
    overlap = ((cs <= ss + SEL_BLOCK - 1) & (cs + CMP_BLOCK - 1 >= ss) & (jnp.arange(NC)[:, None] < NC - 1)
               & (jnp.arange(LANES)[None, :] < S // SEL_BLOCK)).astype(BF16)
    oc, selbias = cmp_attention(proj, kvc[:, 0].transpose(0, 1, 3, 2), kvc[:, 1], overlap, B, S, c_q)

    keys_t = lambda br: kv[:, :, br, 0].transpose(0, 2, 3, 1)
    ones_col = jnp.concatenate([jnp.ones((B, G, S, 1), BF16), jnp.zeros((B, G, S, LANES - dh - 1), BF16)], axis=-1)
    vals_aug = lambda br: jnp.concatenate([kv[:, :, br, 1].transpose(0, 2, 1, 3), ones_col], axis=-1)
    onehot_t = (jnp.arange(LANES)[:, None] == jnp.arange(S)[None, :] // SEL_BLOCK).astype(BF16)
    ksa_t = jnp.concatenate([jnp.broadcast_to(onehot_t, (B, G, LANES, S)), keys_t(1)], axis=2)
    o = nsa_attention(proj, selbias, ksa_t, vals_aug(1), keys_t(2), vals_aug(2), oc, B, S, c_q, col_ng)

    x1, h2, idx, wts = out_proj(o, za, proj, x2, g1, norm2_g, sc2, sh2, w_nsa_out, w_out, w_router, router_bias,
                                S, col_merge + D)

    plan = _dispatch_plan(idx[:, :TOP_K], T, EXPERT_BLOCK)
    y8 = routed_experts(h2, plan, w_gate, w_up, w_down, T * TOP_K)
    out = combine(y8, h2, x1, wts, g2, final_g, w_sh_gate, w_sh_up, w_sh_down, S)
    return out.reshape(B, S, D)


def kernel(x, c, w_ada, b_ada, norm1_g, w_in, b_in, conv_w, conv_b, conv_ln_g, conv_ln_b, w_conv_out, b_conv_out, cmp_pe_k, cmp_w1_k, cmp_w2_k, cmp_pe_v, cmp_w1_v, cmp_w2_v, w_nsa_out, w_out, norm2_g, w_router, router_bias, w_sh_gate, w_sh_up, w_sh_down, w_gate, w_up, w_down, final_g):
    assert w_ada.shape[0] == 1, "single-layer block"
    return _layer(x, c, w_ada[0], b_ada[0], norm1_g[0], w_in[0], b_in[0], conv_w[0], conv_b[0], conv_ln_g[0],
                  conv_ln_b[0], w_conv_out[0], b_conv_out[0], cmp_pe_k[0], cmp_w1_k[0], cmp_w2_k[0], cmp_pe_v[0],
                  cmp_w1_v[0], cmp_w2_v[0], w_nsa_out[0], w_out[0], norm2_g[0], w_router[0], router_bias[0],
                  w_sh_gate[0], w_sh_up[0], w_sh_down[0], w_gate[0], w_up[0], w_down[0], final_g)
```

```python
import functools

import jax
import jax.numpy as jnp
from jax import lax
from jax.experimental import pallas as pl
from jax.experimental.pallas import tpu as pltpu

BF16 = jnp.bfloat16
F32 = jnp.float32
I32 = jnp.int32

D_CONV = 512
CONV_K = 31
N_HEADS = 16
N_KV_GROUPS = 4
HEADS_PER_GROUP = 4
HEAD_DIM = 64
CMP_BLOCK = 32
CMP_STRIDE = 16
SEL_BLOCK = 64
SEL_TOPK = 16
WINDOW = 512
FORCED_SCORE = 1e4
N_EXPERTS = 256
TOP_K = 8
N_GROUPS = 8
TOPK_GROUPS = 4
ROUTED_SCALE = 2.5
EPS = 1e-6
NEG = -1e30
MASK_BIAS = -1e9
LANES = 128
HALO = 32
ATT_TILE = 256
EXPERT_BLOCK = 256
COMBINE_TILE = 256
VMEM_LIMIT = 56 * 1024 * 1024


def _params(*sem):
    return pltpu.CompilerParams(dimension_semantics=sem, vmem_limit_bytes=VMEM_LIMIT)


def _dot(a, b):
    return jnp.dot(a, b, preferred_element_type=F32)


def _split_bf16(x):
    hi = x.astype(BF16)
    lo = (x - hi.astype(F32)).astype(BF16)
    return hi, lo


def _ada_kernel(c_ref, w_ref, b_ref, o_ref):
    c = c_ref[...]
    a = c * jax.nn.sigmoid(c)
    o_ref[...] = _dot(a.astype(BF16), w_ref[...].astype(BF16)) + b_ref[...]


def ada_modulation(c, w_ada, b_ada):
    B, D = c.shape
    N = w_ada.shape[1]
    rows = 8
    c8 = jnp.zeros((rows, D), F32).at[:B].set(c)
    tn = 1024
    out = pl.pallas_call(
        _ada_kernel,
        grid=(N // tn,),
        in_specs=[pl.BlockSpec((rows, D), lambda j: (0, 0)),
                  pl.BlockSpec((D, tn), lambda j: (0, j)),
                  pl.BlockSpec((1, tn), lambda j: (0, j))],
        out_specs=pl.BlockSpec((rows, tn), lambda j: (0, j)),
        out_shape=jax.ShapeDtypeStruct((rows, N), F32),
        compiler_params=_params("arbitrary"),
        name="ada",
    )(c8, w_ada, b_ada.reshape(1, N))
    return out[:B]


def _modulated_rmsnorm(x, g, sc, sh):
    y = x * lax.rsqrt(jnp.mean(x * x, axis=-1, keepdims=True) + EPS)
    return (y * g) * (1.0 + sc) + sh


def _in_proj_kernel(x_ref, g_ref, sc_ref, sh_ref, w_ref, b_ref, o_ref, h_scr):
    @pl.when(pl.program_id(1) == 0)
    def _():
        h = _modulated_rmsnorm(x_ref[...], g_ref[...], sc_ref[...], sh_ref[...])
        h_scr[...] = h.astype(BF16)

    o_ref[...] = (_dot(h_scr[...], w_ref[...]) + b_ref[...]).astype(BF16)


def in_proj(x2, norm_g, sc, sh, w, b, seq, tm=1024, tn=1152):
    T, D = x2.shape
    NP = w.shape[1]
    per_b = seq // tm
    return pl.pallas_call(
        _in_proj_kernel,
        grid=(T // tm, NP // tn),
        in_specs=[pl.BlockSpec((tm, D), lambda i, j: (i, 0)),
                  pl.BlockSpec((1, D), lambda i, j: (0, 0)),
                  pl.BlockSpec((None, 1, D), lambda i, j: (i // per_b, 0, 0)),
                  pl.BlockSpec((None, 1, D), lambda i, j: (i // per_b, 0, 0)),
                  pl.BlockSpec((D, tn), lambda i, j: (0, j)),
                  pl.BlockSpec((1, tn), lambda i, j: (0, j))],
        out_specs=pl.BlockSpec((tm, tn), lambda i, j: (i, j)),
        out_shape=jax.ShapeDtypeStruct((T, NP), BF16),
        scratch_shapes=[pltpu.VMEM((tm, D), BF16)],
        compiler_params=_params("arbitrary", "arbitrary"),
        name="in_proj",
    )(x2, norm_g.reshape(1, D), sc[:, None, :], sh[:, None, :], w, b.reshape(1, NP))


def _conv_kernel(a_ref, gt_ref, pa_ref, pg_ref, mg_ref, cw_ref, cb_ref, lg_ref, lb_ref, wo_ref, bo_ref,
                 o_ref, u_scr, c_scr, *, ts, chunk):
    i = pl.program_id(1)
    a = a_ref[...].astype(F32)
    u_scr[HALO:, :] = a * jax.nn.sigmoid(gt_ref[...].astype(F32))
    pa = pa_ref[...].astype(F32)
    prev = pa * jax.nn.sigmoid(pg_ref[...].astype(F32))
    u_scr[:HALO, :] = jnp.where(i > 0, prev, 0.0)
    off = HALO - (CONV_K - 1)
    for r0 in range(0, ts, chunk):
        acc = jnp.broadcast_to(cb_ref[...], (chunk, D_CONV))
        for k in range(CONV_K):
            acc = acc + cw_ref[k:k + 1, :] * u_scr[r0 + off + k:r0 + off + k + chunk, :]
        c_scr[r0:r0 + chunk, :] = acc
    v = c_scr[...]
    mu = jnp.mean(v, axis=-1, keepdims=True)
    var = jnp.mean(jnp.square(v - mu), axis=-1, keepdims=True)
    y = (v - mu) * lax.rsqrt(var + EPS) * lg_ref[...] + lb_ref[...]
    y = y * jax.nn.sigmoid(y)
    ya = _dot(y.astype(BF16), wo_ref[...]) + bo_ref[...]
    o_ref[...] = (jax.nn.sigmoid(mg_ref[...].astype(F32)) * ya).astype(BF16)


def conformer_conv(proj, B, S, D, merge_col, conv_w, conv_b, ln_g, ln_b, w_o, b_o, ts=512, chunk=64):
    T = B * S
    nt = S // ts
    hb = ts // HALO
    prev_idx = lambda b, i: (jnp.maximum((b * nt + i) * hb - 1, 0), 0)
    prev_idx_g = lambda b, i: (jnp.maximum((b * nt + i) * hb - 1, 0), 1)
    row = lambda v: v.reshape(1, -1)
    return pl.pallas_call(
        functools.partial(_conv_kernel, ts=ts, chunk=chunk),
        grid=(B, nt),
        in_specs=[pl.BlockSpec((ts, D_CONV), lambda b, i: (b * nt + i, 0)),
                  pl.BlockSpec((ts, D_CONV), lambda b, i: (b * nt + i, 1)),
                  pl.BlockSpec((HALO, D_CONV), prev_idx),
                  pl.BlockSpec((HALO, D_CONV), prev_idx_g),
                  pl.BlockSpec((ts, D), lambda b, i: (b * nt + i, merge_col // D)),
                  pl.BlockSpec((CONV_K, D_CONV), lambda b, i: (0, 0)),
                  pl.BlockSpec((1, D_CONV), lambda b, i: (0, 0)),
                  pl.BlockSpec((1, D_CONV), lambda b, i: (0, 0)),
                  pl.BlockSpec((1, D_CONV), lambda b, i: (0, 0)),
                  pl.BlockSpec((D_CONV, D), lambda b, i: (0, 0)),
                  pl.BlockSpec((1, D), lambda b, i: (0, 0))],
        out_specs=pl.BlockSpec((ts, D), lambda b, i: (b * nt + i, 0)),
        out_shape=jax.ShapeDtypeStruct((T, D), BF16),
        scratch_shapes=[pltpu.VMEM((ts + HALO, D_CONV), F32), pltpu.VMEM((ts, D_CONV), F32)],
        compiler_params=_params("arbitrary", "arbitrary"),
        name="conv",
    )(proj, proj, proj, proj, proj, conv_w, row(conv_b), row(ln_g), row(ln_b), w_o.astype(BF16), row(b_o))


def _compress_kernel(z_ref, pe_ref, w1_ref, w2_ref, o_ref, *, nc):
    half = w1_ref.shape[0] // 2
    z = z_ref[...]
    first = _dot(z, w1_ref[:half, :])
    second = _dot(z, w1_ref[half:, :])
    bias = _dot(pe_ref[...], w1_ref[...])[0:1, :]
    hid = first + pltpu.roll(second, nc - 1, 0) + bias
    act = jax.nn.gelu(hid)
    out = _dot(act.astype(BF16), w2_ref[...])
    rows = lax.broadcasted_iota(I32, out.shape, 0)
    o_ref[...] = jnp.where(rows < nc - 1, out, 0.0).astype(BF16)


def compress_kv(zc, pe, w1, w2):
    B, _, G, NC, W = zc.shape
    H = w1.shape[-1]
    pe8 = jnp.broadcast_to(pe.reshape(2, 1, 2 * W), (2, 8, 2 * W)).astype(BF16)
    return pl.pallas_call(
        functools.partial(_compress_kernel, nc=NC),
        grid=(B, 2, G),
        in_specs=[pl.BlockSpec((None, None, None, NC, W), lambda b, s, g: (b, s, g, 0, 0)),
                  pl.BlockSpec((None, 8, 2 * W), lambda b, s, g: (s, 0, 0)),
                  pl.BlockSpec((None, 2 * W, H), lambda b, s, g: (s, 0, 0)),
                  pl.BlockSpec((None, H, HEAD_DIM), lambda b, s, g: (s, 0, 0))],
        out_specs=pl.BlockSpec((None, None, None, NC, HEAD_DIM), lambda b, s, g: (b, s, g, 0, 0)),
        out_shape=jax.ShapeDtypeStruct((B, 2, G, NC, HEAD_DIM), BF16),
        compiler_params=_params("arbitrary", "arbitrary", "arbitrary"),
        name="compress",
    )(zc, pe8, w1.astype(BF16), w2.astype(BF16))


def _cmp_attn_kernel(q_ref, kct_ref, vc_ref, ov_ref, oc_ref, sb_ref, qa_scr, imp_scr, *, tq, ns):
    i = pl.program_id(2)
    t0 = i * tq
    HG = HEADS_PER_GROUP
    R = HG * tq
    nc = kct_ref.shape[1]
    q = q_ref[...] * (HEAD_DIM ** -0.5)
    for h in range(HG):
        qa_scr[h * tq:(h + 1) * tq, :] = q[:, h * HEAD_DIM:(h + 1) * HEAD_DIM]

    def attend(width):
        t_ids = t0 + (lax.broadcasted_iota(I32, (R, width), 0) & (tq - 1))
        n_ids = lax.broadcasted_iota(I32, (R, width), 1)
        valid = (CMP_STRIDE * n_ids + (CMP_BLOCK - 1)) <= t_ids
        s = jnp.where(valid, _dot(qa_scr[...], kct_ref[:, :width]), NEG)
        m = jnp.max(s, axis=-1, keepdims=True)
        e = jnp.where(valid, jnp.exp(s - m), 0.0)
        den = jnp.sum(e, axis=-1, keepdims=True)
        p = e * jnp.where(den > 0.0, 1.0 / den, 0.0)
        o = _dot(p.astype(BF16), vc_ref[:width, :])
        for h in range(HG):
            oc_ref[:, h * HEAD_DIM:(h + 1) * HEAD_DIM] = o[h * tq:(h + 1) * tq, :].astype(BF16)
        psum = p[:tq]
        for h in range(1, HG):
            psum = psum + p[h * tq:(h + 1) * tq]
        ps_hi, ps_lo = _split_bf16(psum)
        imp_scr[...] = _dot(ps_hi, ov_ref[:width, :]) + _dot(ps_lo, ov_ref[:width, :])

    visible = (t0 + tq - (CMP_BLOCK - CMP_STRIDE)) // CMP_STRIDE
    n_tiles = jnp.clip((visible + LANES - 1) // LANES, 1, nc // LANES)
    for k in range(1, nc // LANES + 1):
        pl.when(n_tiles == k)(functools.partial(attend, k * LANES))

    imp = imp_scr[...]
    lane = lax.broadcasted_iota(I32, (tq, LANES), 1)
    jq = (t0 + lax.broadcasted_iota(I32, (tq, LANES), 0)) >> (SEL_BLOCK.bit_length() - 1)
    forced = (lane == 0) | (lane == jq) | (lane == jq - 1)
    imp = jnp.where(forced, FORCED_SCORE, jnp.where(lane <= jq, imp, -1.0))
    work = jnp.where(lane < ns, imp, -jnp.inf).T
    blk_f = lax.broadcasted_iota(I32, (LANES, tq), 0).astype(F32)
    sel = jnp.zeros((LANES, tq), F32)
    for _ in range(SEL_TOPK):
        top = jnp.max(work, axis=0, keepdims=True)
        first = jnp.min(jnp.where(work == top, blk_f, float(LANES)), axis=0, keepdims=True)
        pick = blk_f == first
        sel = jnp.where(pick, 1.0, sel)
        work = jnp.where(pick, -jnp.inf, work)
    sb_ref[...] = jnp.where((sel.T > 0.5) & (lane <= jq), 0.0, MASK_BIAS).astype(BF16)


def cmp_attention(proj, kc_t, vc, overlap, B, S, q_col, tq=ATT_TILE):
    G = N_KV_GROUPS
    NC = vc.shape[2]
    nt = S // tq
    gw = HEADS_PER_GROUP * HEAD_DIM
    ns = S // SEL_BLOCK
    assert SEL_TOPK <= ns <= LANES and tq & (tq - 1) == 0 and NC % LANES == 0
    return pl.pallas_call(
        functools.partial(_cmp_attn_kernel, tq=tq, ns=ns),
        grid=(B, G, nt),
        in_specs=[pl.BlockSpec((tq, gw), lambda b, g, i: (b * nt + i, q_col // gw + g)),
                  pl.BlockSpec((None, None, HEAD_DIM, NC), lambda b, g, i: (b, g, 0, 0)),
                  pl.BlockSpec((None, None, NC, HEAD_DIM), lambda b, g, i: (b, g, 0, 0)),
                  pl.BlockSpec((NC, LANES), lambda b, g, i: (0, 0))],
        out_specs=[pl.BlockSpec((tq, gw), lambda b, g, i: (b * nt + i, g)),
                   pl.BlockSpec((None, None, tq, LANES), lambda b, g, i: (b, g, i, 0))],
        out_shape=[jax.ShapeDtypeStruct((B * S, G * gw), BF16),
                   jax.ShapeDtypeStruct((B, G, S, LANES), BF16)],
        scratch_shapes=[pltpu.VMEM((HEADS_PER_GROUP * tq, HEAD_DIM), BF16), pltpu.VMEM((tq, LANES), F32)],
        compiler_params=_params("arbitrary", "arbitrary", "arbitrary"),
        name="cmp_attn",
    )(proj, kc_t, vc, overlap)


def _nsa_kernel(q_ref, sb_ref, ksa_ref, vs_ref, kw_ref, vw_ref, oc_ref, ng_ref, o_ref,
                qa_scr, ms_scr, as_scr, mw_scr, aw_scr, *, tq, unroll):
    g = pl.program_id(1)
    i = pl.program_id(2)
    HG = HEADS_PER_GROUP
    R = HG * tq
    q = q_ref[...] * (HEAD_DIM ** -0.5)
    sb = sb_ref[...]
    for h in range(HG):
        qa_scr[h * tq:(h + 1) * tq, :LANES] = sb
        qa_scr[h * tq:(h + 1) * tq, LANES:] = q[:, h * HEAD_DIM:(h + 1) * HEAD_DIM]

    q_loc = lax.broadcasted_iota(I32, (R, tq), 0) & (tq - 1)
    k_loc = lax.broadcasted_iota(I32, (R, tq), 1)
    causal = lambda: k_loc <= q_loc
    strictly_upper = lambda: k_loc > q_loc

    def step(kt, mask, k_ref, v_ref, q_lo, m_scr, acc_scr):
        start = pl.multiple_of(kt * tq, tq)
        s = _dot(qa_scr[:, q_lo:], k_ref[:, pl.ds(start, tq)])
        if mask is not None:
            s = jnp.where(mask(), s, NEG)
        m_prev = m_scr[...]
        m_new = jnp.maximum(m_prev, jnp.max(s, axis=-1, keepdims=True))
        alpha = jnp.exp(m_prev - m_new)
        p = jnp.concatenate([jnp.exp(s[:, j * LANES:(j + 1) * LANES] - m_new) for j in range(tq // LANES)], axis=-1)
        acc_scr[...] = alpha * acc_scr[...] + _dot(p.astype(BF16), v_ref[pl.ds(start, tq), :])
        m_scr[...] = m_new

    sel = functools.partial(step, k_ref=ksa_ref, v_ref=vs_ref, q_lo=0, m_scr=ms_scr, acc_scr=as_scr)
    win = functools.partial(step, k_ref=kw_ref, v_ref=vw_ref, q_lo=LANES, m_scr=mw_scr, acc_scr=aw_scr)

    for m_scr, acc_scr in ((ms_scr, as_scr), (mw_scr, aw_scr)):
        m_scr[...] = jnp.full(m_scr.shape, NEG, F32)
        acc_scr[...] = jnp.zeros(acc_scr.shape, F32)

    def sel_unrolled(kp, carry):
        for u in range(unroll):
            sel(unroll * kp + u, None)
        return carry

    def sel_single(kt, carry):
        sel(kt, None)
        return carry

    lax.fori_loop(0, i // unroll, sel_unrolled, 0)
    lax.fori_loop((i // unroll) * unroll, i, sel_single, 0)

    @pl.when(i >= 2)
    def _():
        win(i - 2, strictly_upper)
        sel(i, causal)
        win(i - 1, None)
        win(i, causal)

    @pl.when(i < 2)
    def _():
        sel(i, causal)

        @pl.when(i == 1)
        def _():
            win(0, None)

        win(i, causal)

    acc_s = as_scr[...]
    o_sel = acc_s[:, :HEAD_DIM] / acc_s[:, HEAD_DIM:HEAD_DIM + 1]
    acc_w = aw_scr[...]
    o_win = acc_w[:, :HEAD_DIM] / acc_w[:, HEAD_DIM:HEAD_DIM + 1]

    gates = jax.nn.sigmoid(ng_ref[...].astype(F32))
    oc = oc_ref[...].astype(F32)
    lane = lax.broadcasted_iota(I32, gates.shape, 1)
    for h in range(HG):
        c0 = (g * HG + h) * 3
        gc = jnp.sum(jnp.where(lane == c0, gates, 0.0), axis=-1, keepdims=True)
        gs = jnp.sum(jnp.where(lane == c0 + 1, gates, 0.0), axis=-1, keepdims=True)
        gw = jnp.sum(jnp.where(lane == c0 + 2, gates, 0.0), axis=-1, keepdims=True)
        rows = slice(h * tq, (h + 1) * tq)
        cols = slice(h * HEAD_DIM, (h + 1) * HEAD_DIM)
        o_ref[:, cols] = (gc * oc[:, cols] + gs * o_sel[rows, :] + gw * o_win[rows, :]).astype(BF16)


def nsa_attention(proj, selbias, ksa_t, vs_aug, kw_t, vw_aug, oc, B, S, q_col, ng_col, tq=ATT_TILE, unroll=4):
    G = N_KV_GROUPS
    nt = S // tq
    gw = HEADS_PER_GROUP * HEAD_DIM
    assert WINDOW == 2 * tq and tq & (tq - 1) == 0
    R = HEADS_PER_GROUP * tq
    per_group = lambda rows, cols: pl.BlockSpec((None, None, rows, cols), lambda b, g, i: (b, g, 0, 0))
    return pl.pallas_call(
        functools.partial(_nsa_kernel, tq=tq, unroll=unroll),
        grid=(B, G, nt),
        in_specs=[pl.BlockSpec((tq, gw), lambda b, g, i: (b * nt + i, q_col // gw + g)),
                  pl.BlockSpec((None, None, tq, LANES), lambda b, g, i: (b, g, i, 0)),
                  per_group(LANES + HEAD_DIM, S), per_group(S, LANES), per_group(HEAD_DIM, S), per_group(S, LANES),
                  pl.BlockSpec((tq, gw), lambda b, g, i: (b * nt + i, g)),
                  pl.BlockSpec((tq, LANES), lambda b, g, i: (b * nt + i, ng_col // LANES))],
        out_specs=pl.BlockSpec((tq, gw), lambda b, g, i: (b * nt + i, g)),
        out_shape=jax.ShapeDtypeStruct((B * S, G * gw), BF16),
        scratch_shapes=[pltpu.VMEM((R, LANES + HEAD_DIM), BF16),
                        pltpu.VMEM((R, LANES), F32), pltpu.VMEM((R, LANES), F32),
                        pltpu.VMEM((R, LANES), F32), pltpu.VMEM((R, LANES), F32)],
        compiler_params=_params("arbitrary", "arbitrary", "arbitrary"),
        name="nsa_attn",
    )(proj, selbias, ksa_t, vs_aug, kw_t, vw_aug, oc, proj)


def _route(logits, rbias_col):
    tm, E = logits.shape
    gsz = E // N_GROUPS
    lt = logits.T
    scores = jax.nn.sigmoid(lt)
    sfc = scores + rbias_col
    row_f = lax.broadcasted_iota(I32, (E, tm), 0).astype(F32)
    gscore = []
    for g in range(N_GROUPS):
        v = sfc[g * gsz:(g + 1) * gsz]
        m1 = jnp.max(v, axis=0, keepdims=True)
        cnt = jnp.sum(jnp.where(v == m1, 1.0, 0.0), axis=0, keepdims=True)
        m2 = jnp.max(jnp.where(v < m1, v, -jnp.inf), axis=0, keepdims=True)
        gscore.append(m1 + jnp.where(cnt >= 2.0, m1, m2))
    masked = []
    for g in range(N_GROUPS):
        rank = jnp.zeros((1, tm), F32)
        for o in range(N_GROUPS):
            if o == g:
                continue
            ahead = (gscore[o] > gscore[g]) | ((gscore[o] == gscore[g]) & (o < g))
            rank = rank + jnp.where(ahead, 1.0, 0.0)
        masked.append(jnp.where(rank < float(TOPK_GROUPS), sfc[g * gsz:(g + 1) * gsz], NEG))
    work = jnp.concatenate(masked, axis=0)
    out_row = lax.broadcasted_iota(I32, (LANES, tm), 0)
    idx_out = jnp.zeros((LANES, tm), F32)
    w_out = jnp.zeros((LANES, tm), F32)
    wsum = jnp.zeros((1, tm), F32)
    for r in range(TOP_K):
        top = jnp.max(work, axis=0, keepdims=True)
        first = jnp.min(jnp.where(work == top, row_f, float(E)), axis=0, keepdims=True)
        pick = row_f == first
        w = jnp.sum(jnp.where(pick, scores, 0.0), axis=0, keepdims=True)
        idx_out = jnp.where(out_row == r, first, idx_out)
        w_out = jnp.where(out_row == r, w, w_out)
        wsum = wsum + w
        work = jnp.where(pick, -jnp.inf, work)
    return idx_out.T.astype(I32), (w_out / wsum * ROUTED_SCALE).T


def _out_proj_kernel(o_ref, za_ref, mb_ref, x_ref, g1_ref, n2_ref, sc_ref, sh_ref, wn_ref, wo_ref,
                     wrh_ref, wrl_ref, rb_ref, x1_ref, h2_ref, idx_ref, wt_ref):
    yb = _dot(o_ref[...], wn_ref[...])
    z = za_ref[...].astype(F32) + jax.nn.sigmoid(mb_ref[...].astype(F32)) * yb
    mix = _dot(z.astype(BF16), wo_ref[...])
    x1 = x_ref[...] + g1_ref[...] * mix
    x1_ref[...] = x1
    h2 = _modulated_rmsnorm(x1, n2_ref[...], sc_ref[...], sh_ref[...])
    h2_ref[...] = h2
    hi, lo = _split_bf16(h2)
    logits = _dot(hi, wrh_ref[...]) + (_dot(lo, wrh_ref[...]) + _dot(hi, wrl_ref[...]))
    idx, wts = _route(logits, rb_ref[...])
    idx_ref[...] = idx
    wt_ref[...] = wts


def out_proj(o, za, proj, x2, g1, norm_g, sc, sh, w_nsa_out, w_out, w_router, router_bias, seq, mb_col, tm=256):
    T, D = x2.shape
    E = w_router.shape[1]
    per_b = seq // tm
    wr_hi, wr_lo = _split_bf16(w_router)
    tok = lambda i: (i, 0)
    const = lambda i: (0, 0)
    per_batch = pl.BlockSpec((None, 1, D), lambda i: (i // per_b, 0, 0))
    return pl.pallas_call(
        _out_proj_kernel,
        grid=(T // tm,),
        in_specs=[pl.BlockSpec((tm, D), tok), pl.BlockSpec((tm, D), tok),
                  pl.BlockSpec((tm, D), lambda i: (i, mb_col // D)),
                  pl.BlockSpec((tm, D), tok), per_batch,
                  pl.BlockSpec((1, D), const), per_batch, per_batch,
                  pl.BlockSpec((D, D), const), pl.BlockSpec((D, D), const),
                  pl.BlockSpec((D, E), const), pl.BlockSpec((D, E), const), pl.BlockSpec((E, 1), const)],
        out_specs=[pl.BlockSpec((tm, D), tok), pl.BlockSpec((tm, D), tok),
                   pl.BlockSpec((tm, LANES), tok), pl.BlockSpec((tm, LANES), tok)],
        out_shape=[jax.ShapeDtypeStruct((T, D), F32), jax.ShapeDtypeStruct((T, D), F32),
                   jax.ShapeDtypeStruct((T, LANES), I32), jax.ShapeDtypeStruct((T, LANES), F32)],
        compiler_params=_params("arbitrary"),
        name="out_proj",
    )(o, za, proj, x2, g1[:, None, :], norm_g.reshape(1, D), sc[:, None, :], sh[:, None, :],
      w_nsa_out.astype(BF16), w_out.astype(BF16), wr_hi, wr_lo, router_bias.reshape(E, 1))


def _experts_kernel(ex_ref, ni_ref, gidx_hbm, sidx_hbm, h_hbm, wg0, wu0, wd0, wg1, wu1, wd1, y_hbm,
                    g_idx, s_idx, x_buf, y_buf, g_sem, s_sem, row_sem, out_sem, *, bm, dump_row):
    n = pl.program_id(0)
    n_items = ni_ref[0]
    w0 = 2 * n
    w1 = w0 + 1

    def idx_copy(src_hbm, w, smem, sem, s):
        return pltpu.make_async_copy(src_hbm.at[w], smem.at[s], sem.at[s])

    gather_idx = functools.partial(idx_copy, gidx_hbm, smem=g_idx, sem=g_sem)
    scatter_idx = functools.partial(idx_copy, sidx_hbm, smem=s_idx, sem=s_sem)

    def issue_gather(s):
        for r in range(bm):
            pltpu.make_async_copy(h_hbm.at[pl.ds(g_idx[s, r], 1), :], x_buf.at[s, pl.ds(r, 1), :],
                                  row_sem.at[s]).start(priority=r % 2)

    def issue_scatter(s):
        for r in range(bm):
            pltpu.make_async_copy(y_buf.at[s, pl.ds(r, 1), :], y_hbm.at[pl.ds(s_idx[s, r], 1), :],
                                  out_sem.at[s]).start(priority=r % 2)

    def wait_gather(s):
        pltpu.make_async_copy(h_hbm.at[pl.ds(0, bm), :], x_buf.at[s], row_sem.at[s]).wait()

    def wait_scatter(s):
        pltpu.make_async_copy(y_buf.at[s], y_hbm.at[pl.ds(0, bm), :], out_sem.at[s]).wait()

    def expert(s, wg, wu, wd):
        x = x_buf[s].astype(BF16)
        gate = _dot(x, wg[...].astype(BF16))
        up = _dot(x, wu[...].astype(BF16))
        hid = gate * jax.nn.sigmoid(gate) * up
        y_buf[s] = _dot(hid.astype(BF16), wd[...].astype(BF16))

    @pl.when(n == 0)
    def _():
        y_buf[...] = jnp.zeros(y_buf.shape, F32)
        for s in range(2):
            pltpu.make_async_copy(y_buf.at[s], y_hbm.at[pl.ds(dump_row + s * bm, bm), :], out_sem.at[s]).start()
        gather_idx(w=0, s=0).start()
        gather_idx(w=1, s=1).start()
        gather_idx(w=0, s=0).wait()
        issue_gather(0)

    @pl.when(w0 < n_items)
    def _():
        gather_idx(w=w0 + 2, s=0).start()
        scatter_idx(w=w0, s=0).start()
        scatter_idx(w=w1, s=1).start()
        gather_idx(w=w1, s=1).wait()
        issue_gather(1)
        gather_idx(w=w1 + 2, s=1).start()
        wait_gather(0)
        wait_scatter(0)
        expert(0, wg0, wu0, wd0)
        scatter_idx(w=w0, s=0).wait()
        issue_scatter(0)
        gather_idx(w=w0 + 2, s=0).wait()
        issue_gather(0)
        wait_gather(1)
        wait_scatter(1)
        expert(1, wg1, wu1, wd1)
        scatter_idx(w=w1, s=1).wait()
        issue_scatter(1)

        @pl.when(w0 + 2 >= n_items)
        def _():
            wait_gather(0)
            gather_idx(w=w1 + 2, s=1).wait()
            wait_scatter(0)
            wait_scatter(1)


def routed_experts(h2, plan, w_gate, w_up, w_down, n_assign, bm=EXPERT_BLOCK):
    ex, n_items, gidx, sidx = plan
    T, D = h2.shape
    E, _, DE = w_gate.shape
    n_work = ex.shape[0] - 2
    even = lambda n, ex, ni: (ex[2 * n], 0, 0)
    odd = lambda n, ex, ni: (ex[2 * n + 1], 0, 0)
    grid_spec = pltpu.PrefetchScalarGridSpec(
        num_scalar_prefetch=2,
        grid=(n_work // 2,),
        in_specs=[pl.BlockSpec(memory_space=pl.ANY), pl.BlockSpec(memory_space=pl.ANY),
                  pl.BlockSpec(memory_space=pl.ANY),
                  pl.BlockSpec((None, D, DE), even), pl.BlockSpec((None, D, DE), even), pl.BlockSpec((None, DE, D), even),
                  pl.BlockSpec((None, D, DE), odd), pl.BlockSpec((None, D, DE), odd), pl.BlockSpec((None, DE, D), odd)],
        out_specs=pl.BlockSpec(memory_space=pl.ANY),
        scratch_shapes=[pltpu.SMEM((2, bm), I32), pltpu.SMEM((2, bm), I32),
                        pltpu.VMEM((2, bm, D), F32), pltpu.VMEM((2, bm, D), F32),
                        pltpu.SemaphoreType.DMA((2,)), pltpu.SemaphoreType.DMA((2,)),
                        pltpu.SemaphoreType.DMA((2,)), pltpu.SemaphoreType.DMA((2,))],
    )
    return pl.pallas_call(
        functools.partial(_experts_kernel, bm=bm, dump_row=n_assign),
        grid_spec=grid_spec,
        out_shape=jax.ShapeDtypeStruct((n_assign + 2 * bm, D), F32),
        compiler_params=_params("arbitrary"),
        name="experts",
    )(ex, n_items, gidx, sidx, h2, w_gate, w_up, w_down, w_gate, w_up, w_down)


def _combine_kernel(*refs):
    y8_refs = refs[:TOP_K]
    h2_ref, x1_ref, wt_ref, g2_ref, fg_ref, wsg_ref, wsu_ref, wsd_ref, o_ref = refs[TOP_K:]
    h2 = h2_ref[...].astype(BF16)
    gate = _dot(h2, wsg_ref[...])
    up = _dot(h2, wsu_ref[...])
    y = _dot((gate * jax.nn.sigmoid(gate) * up).astype(BF16), wsd_ref[...])
    wt = wt_ref[...]
    for k in range(TOP_K):
        y = y + wt[:, k:k + 1] * y8_refs[k][...]
    x2 = x1_ref[...] + g2_ref[...] * y
    o_ref[...] = x2 * lax.rsqrt(jnp.mean(x2 * x2, axis=-1, keepdims=True) + EPS) * fg_ref[...]


def combine(y8, h2, x1, wts, g2, final_g, w_sh_gate, w_sh_up, w_sh_down, seq, tj=COMBINE_TILE):
    T, D = x1.shape
    DS = w_sh_gate.shape[1]
    per_b = seq // tj
    nt = T // tj
    tok = lambda i: (i, 0)
    const = lambda i: (0, 0)
    slot = lambda k: pl.BlockSpec((tj, D), lambda i: (k * nt + i, 0))
    return pl.pallas_call(
        _combine_kernel,
        grid=(nt,),
        in_specs=[slot(k) for k in range(TOP_K)] + [
                  pl.BlockSpec((tj, D), tok), pl.BlockSpec((tj, D), tok), pl.BlockSpec((tj, LANES), tok),
                  pl.BlockSpec((None, 1, D), lambda i: (i // per_b, 0, 0)),
                  pl.BlockSpec((1, D), const),
                  pl.BlockSpec((D, DS), const), pl.BlockSpec((D, DS), const), pl.BlockSpec((DS, D), const)],
        out_specs=pl.BlockSpec((tj, D), tok),
        out_shape=jax.ShapeDtypeStruct((T, D), F32),
        compiler_params=_params("arbitrary"),
        name="combine",
    )(*([y8] * TOP_K), h2, x1, wts, g2[:, None, :], final_g.reshape(1, D),
      w_sh_gate.astype(BF16), w_sh_up.astype(BF16), w_sh_down.astype(BF16))


def _dispatch_plan(idx, T, bm):
    A = T * TOP_K
    assert A % bm == 0
    nblk = A // bm
    E = N_EXPERTS
    kbits = TOP_K.bit_length() - 1
    sorted_e, order = lax.sort((idx.reshape(A), jnp.arange(A, dtype=I32)), num_keys=1)
    bounds = jnp.searchsorted(sorted_e, jnp.arange(E + 1, dtype=I32), side='left').astype(I32)
    start, end = bounds[:-1], bounds[1:]
    first_blk = start // bm
    n_it = jnp.where(end > start, (end - 1) // bm - first_blk + 1, 0)
    it_end = jnp.cumsum(n_it)
    it_start = it_end - n_it
    n_items = it_end[-1]
    n_work = nblk + E
    w = jnp.arange(n_work + 2, dtype=I32)
    real = w < n_items
    wc = jnp.minimum(w, n_items - 1)
    ex = jnp.searchsorted(it_end, wc, side='right').astype(I32)
    blk = first_blk[ex] + wc - it_start[ex]
    lo = jnp.maximum(start[ex], blk * bm) - blk * bm
    hi = jnp.where(real, jnp.minimum(end[ex], (blk + 1) * bm) - blk * bm, lo)
    rows = order.reshape(nblk, bm)[blk]
    r = jnp.arange(bm, dtype=I32)[None, :]
    mine = (r >= lo[:, None]) & (r < hi[:, None])
    gidx = rows >> kbits
    sidx = jnp.where(mine, (rows & (TOP_K - 1)) * T + gidx, A + (w[:, None] % 2) * bm + r)
    return ex, n_items.astype(I32).reshape(1), gidx.astype(I32), sidx.astype(I32)


def _layer(x, c, w_ada, b_ada, norm1_g, w_in, b_in, conv_w, conv_b, conv_ln_g, conv_ln_b, w_conv_out, b_conv_out,
           cmp_pe_k, cmp_w1_k, cmp_w2_k, cmp_pe_v, cmp_w1_v, cmp_w2_v, w_nsa_out, w_out, norm2_g, w_router,
           router_bias, w_sh_gate, w_sh_up, w_sh_down, w_gate, w_up, w_down, final_g):
    B, S, D = x.shape
    T = B * S
    G, dh = N_KV_GROUPS, HEAD_DIM
    n_q = N_HEADS * dh
    n_kv = 3 * 2 * G * dh
    n_ng = 3 * N_HEADS
    mod = ada_modulation(c, w_ada, b_ada)
    sh1, sc1, g1, sh2, sc2, g2 = jnp.split(mod, 6, axis=-1)

    c_q = 2 * D_CONV
    c_kv = c_q + n_q
    c_ng = c_kv + n_kv
    c_mg = c_ng + n_ng
    pad = LANES - n_ng
    w_r = jnp.concatenate([w_in[:, :c_kv], w_in[:, c_mg:], w_in[:, c_kv:c_mg], jnp.zeros((D, pad), F32)], axis=1)
    b_r = jnp.concatenate([b_in[:c_kv], b_in[c_mg:], b_in[c_kv:c_mg], jnp.zeros((pad,), F32)])
    col_merge = c_kv
    col_kv = col_merge + 2 * D
    col_ng = col_kv + n_kv
    x2 = x.reshape(T, D)
    proj = in_proj(x2, norm1_g, sc1, sh1, w_r.astype(BF16), b_r, S)

    za = conformer_conv(proj, B, S, D, col_merge, conv_w, conv_b, conv_ln_g, conv_ln_b, w_conv_out, b_conv_out)

    kv = proj[:, col_kv:col_kv + n_kv].reshape(B, S, 3, 2, G, dh)
    NC = S // CMP_STRIDE
    zc = kv[:, :, 0].reshape(B, NC, CMP_STRIDE, 2, G, dh).transpose(0, 3, 4, 1, 2, 5).reshape(B, 2, G, NC, CMP_STRIDE * dh)
    kvc = compress_kv(zc, jnp.stack([cmp_pe_k, cmp_pe_v]), jnp.stack([cmp_w1_k, cmp_w1_v]),
                      jnp.stack([cmp_w2_k, cmp_w2_v]))

    cs = CMP_STRIDE * jnp.arange(NC)[:, None]
    ss = SEL_BLOCK * jnp.arange(LANES)[None, :]
    overlap = ((cs <= ss + SEL_BLOCK - 1) & (cs + CMP_BLOCK - 1 >= ss) & (jnp.arange(NC)[:, None] < NC - 1)
               & (jnp.arange(LANES)[None, :] < S // SEL_BLOCK)).astype(BF16)
    oc, selbias = cmp_attention(proj, kvc[:, 0].transpose(0, 1, 3, 2), kvc[:, 1], overlap, B, S, c_q)

    keys_t = lambda br: kv[:, :, br, 0].transpose(0, 2, 3, 1)
    ones_col = jnp.concatenate([jnp.ones((B, G, S, 1), BF16), jnp.zeros((B, G, S, LANES - dh - 1), BF16)], axis=-1)
    vals_aug = lambda br: jnp.concatenate([kv[:, :, br, 1].transpose(0, 2, 1, 3), ones_col], axis=-1)
    onehot_t = (jnp.arange(LANES)[:, None] == jnp.arange(S)[None, :] // SEL_BLOCK).astype(BF16)
    ksa_t = jnp.concatenate([jnp.broadcast_to(onehot_t, (B, G, LANES, S)), keys_t(1)], axis=2)
    o = nsa_attention(proj, selbias, ksa_t, vals_aug(1), keys_t(2), vals_aug(2), oc, B, S, c_q, col_ng)

    x1, h2, idx, wts = out_proj(o, za, proj, x2, g1, norm2_g, sc2, sh2, w_nsa_out, w_out, w_router, router_bias,
                                S, col_merge + D)

    plan = _dispatch_plan(idx[:, :TOP_K], T, EXPERT_BLOCK)
    y8 = routed_experts(h2, plan, w_gate, w_up, w_down, T * TOP_K)
    out = combine(y8, h2, x1, wts, g2, final_g, w_sh_gate, w_sh_up, w_sh_down, S)
    return out.reshape(B, S, D)


def kernel(x, c, w_ada, b_ada, norm1_g, w_in, b_in, conv_w, conv_b, conv_ln_g, conv_ln_b, w_conv_out, b_conv_out, cmp_pe_k, cmp_w1_k, cmp_w2_k, cmp_pe_v, cmp_w1_v, cmp_w2_v, w_nsa_out, w_out, norm2_g, w_router, router_bias, w_sh_gate, w_sh_up, w_sh_down, w_gate, w_up, w_down, final_g):
    assert w_ada.shape[0] == 1, "single-layer block"
    return _layer(x, c, w_ada[0], b_ada[0], norm1_g[0], w_in[0], b_in[0], conv_w[0], conv_b[0], conv_ln_g[0],
                  conv_ln_b[0], w_conv_out[0], b_conv_out[0], cmp_pe_k[0], cmp_w1_k[0], cmp_w2_k[0], cmp_pe_v[0],
                  cmp_w1_v[0], cmp_w2_v[0], w_nsa_out[0], w_out[0], norm2_g[0], w_router[0], router_bias[0],
                  w_sh_gate[0], w_sh_up[0], w_sh_down[0], w_gate[0], w_up[0], w_down[0], final_g)
```

```python
import functools

import jax
import jax.numpy as jnp
from jax import lax
from jax.experimental import pallas as pl
from jax.experimental.pallas import tpu as pltpu

BF16 = jnp.bfloat16
F32 = jnp.float32
I32 = jnp.int32

D_CONV = 512
CONV_K = 31
N_HEADS = 16
N_KV_GROUPS = 4
HEADS_PER_GROUP = 4
HEAD_DIM = 64
CMP_BLOCK = 32
CMP_STRIDE = 16
SEL_BLOCK = 64
SEL_TOPK = 16
WINDOW = 512
FORCED_SCORE = 1e4
N_EXPERTS = 256
TOP_K = 8
N_GROUPS = 8
TOPK_GROUPS = 4
ROUTED_SCALE = 2.5
EPS = 1e-6
NEG = -1e30
MASK_BIAS = -1e9
LANES = 128
HALO = 32
ATT_TILE = 256
EXPERT_BLOCK = 256
COMBINE_TILE = 256
VMEM_LIMIT = 56 * 1024 * 1024


def _params(*sem):
    return pltpu.CompilerParams(dimension_semantics=sem, vmem_limit_bytes=VMEM_LIMIT)


def _dot(a, b):
    return jnp.dot(a, b, preferred_element_type=F32)


def _split_bf16(x):
    hi = x.astype(BF16)
    lo = (x - hi.astype(F32)).astype(BF16)
    return hi, lo


def _ada_kernel(c_ref, w_ref, b_ref, o_ref):
    c = c_ref[...]
    a = c * jax.nn.sigmoid(c)
    o_ref[...] = _dot(a.astype(BF16), w_ref[...].astype(BF16)) + b_ref[...]


def ada_modulation(c, w_ada, b_ada):
    B, D = c.shape
    N = w_ada.shape[1]
    rows = 8
    c8 = jnp.zeros((rows, D), F32).at[:B].set(c)
    tn = 1024
    out = pl.pallas_call(
        _ada_kernel,
        grid=(N // tn,),
        in_specs=[pl.BlockSpec((rows, D), lambda j: (0, 0)),
                  pl.BlockSpec((D, tn), lambda j: (0, j)),
                  pl.BlockSpec((1, tn), lambda j: (0, j))],
        out_specs=pl.BlockSpec((rows, tn), lambda j: (0, j)),
        out_shape=jax.ShapeDtypeStruct((rows, N), F32),
        compiler_params=_params("arbitrary"),
        name="ada",
    )(c8, w_ada, b_ada.reshape(1, N))
    return out[:B]


def _modulated_rmsnorm(x, g, sc, sh):
    y = x * lax.rsqrt(jnp.mean(x * x, axis=-1, keepdims=True) + EPS)
    return (y * g) * (1.0 + sc) + sh


def _in_proj_kernel(x_ref, g_ref, sc_ref, sh_ref, w_ref, b_ref, o_ref, h_scr):
    @pl.when(pl.program_id(1) == 0)
    def _():
        h = _modulated_rmsnorm(x_ref[...], g_ref[...], sc_ref[...], sh_ref[...])
        h_scr[...] = h.astype(BF16)

    o_ref[...] = (_dot(h_scr[...], w_ref[...]) + b_ref[...]).astype(BF16)


def in_proj(x2, norm_g, sc, sh, w, b, seq, tm=1024, tn=1152):
    T, D = x2.shape
    NP = w.shape[1]
    per_b = seq // tm
    return pl.pallas_call(
        _in_proj_kernel,
        grid=(T // tm, NP // tn),
        in_specs=[pl.BlockSpec((tm, D), lambda i, j: (i, 0)),
                  pl.BlockSpec((1, D), lambda i, j: (0, 0)),
                  pl.BlockSpec((None, 1, D), lambda i, j: (i // per_b, 0, 0)),
                  pl.BlockSpec((None, 1, D), lambda i, j: (i // per_b, 0, 0)),
                  pl.BlockSpec((D, tn), lambda i, j: (0, j)),
                  pl.BlockSpec((1, tn), lambda i, j: (0, j))],
        out_specs=pl.BlockSpec((tm, tn), lambda i, j: (i, j)),
        out_shape=jax.ShapeDtypeStruct((T, NP), BF16),
        scratch_shapes=[pltpu.VMEM((tm, D), BF16)],
        compiler_params=_params("arbitrary", "arbitrary"),
        name="in_proj",
    )(x2, norm_g.reshape(1, D), sc[:, None, :], sh[:, None, :], w, b.reshape(1, NP))


def _conv_kernel(a_ref, gt_ref, pa_ref, pg_ref, mg_ref, cw_ref, cb_ref, lg_ref, lb_ref, wo_ref, bo_ref,
                 o_ref, u_scr, c_scr, *, ts, chunk):
    i = pl.program_id(1)
    a = a_ref[...].astype(F32)
    u_scr[HALO:, :] = a * jax.nn.sigmoid(gt_ref[...].astype(F32))
    pa = pa_ref[...].astype(F32)
    prev = pa * jax.nn.sigmoid(pg_ref[...].astype(F32))
    u_scr[:HALO, :] = jnp.where(i > 0, prev, 0.0)
    off = HALO - (CONV_K - 1)
    for r0 in range(0, ts, chunk):
        acc = jnp.broadcast_to(cb_ref[...], (chunk, D_CONV))
        for k in range(CONV_K):
            acc = acc + cw_ref[k:k + 1, :] * u_scr[r0 + off + k:r0 + off + k + chunk, :]
        c_scr[r0:r0 + chunk, :] = acc
    v = c_scr[...]
    mu = jnp.mean(v, axis=-1, keepdims=True)
    var = jnp.mean(jnp.square(v - mu), axis=-1, keepdims=True)
    y = (v - mu) * lax.rsqrt(var + EPS) * lg_ref[...] + lb_ref[...]
    y = y * jax.nn.sigmoid(y)
    ya = _dot(y.astype(BF16), wo_ref[...]) + bo_ref[...]
    o_ref[...] = (jax.nn.sigmoid(mg_ref[...].astype(F32)) * ya).astype(BF16)


def conformer_conv(proj, B, S, D, merge_col, conv_w, conv_b, ln_g, ln_b, w_o, b_o, ts=512, chunk=64):
    T = B * S
    nt = S // ts
    hb = ts // HALO
    prev_idx = lambda b, i: (jnp.maximum((b * nt + i) * hb - 1, 0), 0)
    prev_idx_g = lambda b, i: (jnp.maximum((b * nt + i) * hb - 1, 0), 1)
    row = lambda v: v.reshape(1, -1)
    return pl.pallas_call(
        functools.partial(_conv_kernel, ts=ts, chunk=chunk),
        grid=(B, nt),
        in_specs=[pl.BlockSpec((ts, D_CONV), lambda b, i: (b * nt + i, 0)),
                  pl.BlockSpec((ts, D_CONV), lambda b, i: (b * nt + i, 1)),
                  pl.BlockSpec((HALO, D_CONV), prev_idx),
                  pl.BlockSpec((HALO, D_CONV), prev_idx_g),
                  pl.BlockSpec((ts, D), lambda b, i: (b * nt + i, merge_col // D)),
                  pl.BlockSpec((CONV_K, D_CONV), lambda b, i: (0, 0)),
                  pl.BlockSpec((1, D_CONV), lambda b, i: (0, 0)),
                  pl.BlockSpec((1, D_CONV), lambda b, i: (0, 0)),
                  pl.BlockSpec((1, D_CONV), lambda b, i: (0, 0)),
                  pl.BlockSpec((D_CONV, D), lambda b, i: (0, 0)),
                  pl.BlockSpec((1, D), lambda b, i: (0, 0))],
        out_specs=pl.BlockSpec((ts, D), lambda b, i: (b * nt + i, 0)),
        out_shape=jax.ShapeDtypeStruct((T, D), BF16),
        scratch_shapes=[pltpu.VMEM((ts + HALO, D_CONV), F32), pltpu.VMEM((ts, D_CONV), F32)],
        compiler_params=_params("arbitrary", "arbitrary"),
        name="conv",
    )(proj, proj, proj, proj, proj, conv_w, row(conv_b), row(ln_g), row(ln_b), w_o.astype(BF16), row(b_o))


def _compress_kernel(z_ref, pe_ref, w1_ref, w2_ref, o_ref, *, nc):
    half = w1_ref.shape[0] // 2
    z = z_ref[...]
    first = _dot(z, w1_ref[:half, :])
    second = _dot(z, w1_ref[half:, :])
    bias = _dot(pe_ref[...], w1_ref[...])[0:1, :]
    hid = first + pltpu.roll(second, nc - 1, 0) + bias
    act = jax.nn.gelu(hid)
    out = _dot(act.astype(BF16), w2_ref[...])
    rows = lax.broadcasted_iota(I32, out.shape, 0)
    o_ref[...] = jnp.where(rows < nc - 1, out, 0.0).astype(BF16)


def compress_kv(zc, pe, w1, w2):
    B, _, G, NC, W = zc.shape
    H = w1.shape[-1]
    pe8 = jnp.broadcast_to(pe.reshape(2, 1, 2 * W), (2, 8, 2 * W)).astype(BF16)
    return pl.pallas_call(
        functools.partial(_compress_kernel, nc=NC),
        grid=(B, 2, G),
        in_specs=[pl.BlockSpec((None, None, None, NC, W), lambda b, s, g: (b, s, g, 0, 0)),
                  pl.BlockSpec((None, 8, 2 * W), lambda b, s, g: (s, 0, 0)),
                  pl.BlockSpec((None, 2 * W, H), lambda b, s, g: (s, 0, 0)),
                  pl.BlockSpec((None, H, HEAD_DIM), lambda b, s, g: (s, 0, 0))],
        out_specs=pl.BlockSpec((None, None, None, NC, HEAD_DIM), lambda b, s, g: (b, s, g, 0, 0)),
        out_shape=jax.ShapeDtypeStruct((B, 2, G, NC, HEAD_DIM), BF16),
        compiler_params=_params("arbitrary", "arbitrary", "arbitrary"),
        name="compress",
    )(zc, pe8, w1.astype(BF16), w2.astype(BF16))


def _cmp_attn_kernel(q_ref, kct_ref, vc_ref, ov_ref, oc_ref, sb_ref, qa_scr, imp_scr, *, tq, ns):
    i = pl.program_id(2)
    t0 = i * tq
    HG = HEADS_PER_GROUP
    R = HG * tq
    nc = kct_ref.shape[1]
    q = q_ref[...] * (HEAD_DIM ** -0.5)
    for h in range(HG):
        qa_scr[h * tq:(h + 1) * tq, :] = q[:, h * HEAD_DIM:(h + 1) * HEAD_DIM]

    def attend(width):
        t_ids = t0 + (lax.broadcasted_iota(I32, (R, width), 0) & (tq - 1))
        n_ids = lax.broadcasted_iota(I32, (R, width), 1)
        valid = (CMP_STRIDE * n_ids + (CMP_BLOCK - 1)) <= t_ids
        s = jnp.where(valid, _dot(qa_scr[...], kct_ref[:, :width]), NEG)
        m = jnp.max(s, axis=-1, keepdims=True)
        e = jnp.where(valid, jnp.exp(s - m), 0.0)
        den = jnp.sum(e, axis=-1, keepdims=True)
        p = e * jnp.where(den > 0.0, 1.0 / den, 0.0)
        o = _dot(p.astype(BF16), vc_ref[:width, :])
        for h in range(HG):
            oc_ref[:, h * HEAD_DIM:(h + 1) * HEAD_DIM] = o[h * tq:(h + 1) * tq, :].astype(BF16)
        psum = p[:tq]
        for h in range(1, HG):
            psum = psum + p[h * tq:(h + 1) * tq]
        ps_hi, ps_lo = _split_bf16(psum)
        imp_scr[...] = _dot(ps_hi, ov_ref[:width, :]) + _dot(ps_lo, ov_ref[:width, :])

    visible = (t0 + tq - (CMP_BLOCK - CMP_STRIDE)) // CMP_STRIDE
    n_tiles = jnp.clip((visible + LANES - 1) // LANES, 1, nc // LANES)
    for k in range(1, nc // LANES + 1):
        pl.when(n_tiles == k)(functools.partial(attend, k * LANES))

    imp = imp_scr[...]
    lane = lax.broadcasted_iota(I32, (tq, LANES), 1)
    jq = (t0 + lax.broadcasted_iota(I32, (tq, LANES), 0)) >> (SEL_BLOCK.bit_length() - 1)
    forced = (lane == 0) | (lane == jq) | (lane == jq - 1)
    imp = jnp.where(forced, FORCED_SCORE, jnp.where(lane <= jq, imp, -1.0))
    work = jnp.where(lane < ns, imp, -jnp.inf).T
    blk_f = lax.broadcasted_iota(I32, (LANES, tq), 0).astype(F32)
    sel = jnp.zeros((LANES, tq), F32)
    for _ in range(SEL_TOPK):
        top = jnp.max(work, axis=0, keepdims=True)
        first = jnp.min(jnp.where(work == top, blk_f, float(LANES)), axis=0, keepdims=True)
        pick = blk_f == first
        sel = jnp.where(pick, 1.0, sel)
        work = jnp.where(pick, -jnp.inf, work)
    sb_ref[...] = jnp.where((sel.T > 0.5) & (lane <= jq), 0.0, MASK_BIAS).astype(BF16)


def cmp_attention(proj, kc_t, vc, overlap, B, S, q_col, tq=ATT_TILE):
    G = N_KV_GROUPS
    NC = vc.shape[2]
    nt = S // tq
    gw = HEADS_PER_GROUP * HEAD_DIM
    ns = S // SEL_BLOCK
    assert SEL_TOPK <= ns <= LANES and tq & (tq - 1) == 0 and NC % LANES == 0
    return pl.pallas_call(
        functools.partial(_cmp_attn_kernel, tq=tq, ns=ns),
        grid=(B, G, nt),
        in_specs=[pl.BlockSpec((tq, gw), lambda b, g, i: (b * nt + i, q_col // gw + g)),
                  pl.BlockSpec((None, None, HEAD_DIM, NC), lambda b, g, i: (b, g, 0, 0)),
                  pl.BlockSpec((None, None, NC, HEAD_DIM), lambda b, g, i: (b, g, 0, 0)),
                  pl.BlockSpec((NC, LANES), lambda b, g, i: (0, 0))],
        out_specs=[pl.BlockSpec((tq, gw), lambda b, g, i: (b * nt + i, g)),
                   pl.BlockSpec((None, None, tq, LANES), lambda b, g, i: (b, g, i, 0))],
        out_shape=[jax.ShapeDtypeStruct((B * S, G * gw), BF16),
                   jax.ShapeDtypeStruct((B, G, S, LANES), BF16)],
        scratch_shapes=[pltpu.VMEM((HEADS_PER_GROUP * tq, HEAD_DIM), BF16), pltpu.VMEM((tq, LANES), F32)],
        compiler_params=_params("arbitrary", "arbitrary", "arbitrary"),
        name="cmp_attn",
    )(proj, kc_t, vc, overlap)


def _nsa_kernel(q_ref, sb_ref, ksa_ref, vs_ref, kw_ref, vw_ref, oc_ref, ng_ref, o_ref,
                qa_scr, ms_scr, as_scr, mw_scr, aw_scr, *, tq, unroll):
    g = pl.program_id(1)
    i = pl.program_id(2)
    HG = HEADS_PER_GROUP
    R = HG * tq
    q = q_ref[...] * (HEAD_DIM ** -0.5)
    sb = sb_ref[...]
    for h in range(HG):
        qa_scr[h * tq:(h + 1) * tq, :LANES] = sb
        qa_scr[h * tq:(h + 1) * tq, LANES:] = q[:, h * HEAD_DIM:(h + 1) * HEAD_DIM]

    q_loc = lax.broadcasted_iota(I32, (R, tq), 0) & (tq - 1)
    k_loc = lax.broadcasted_iota(I32, (R, tq), 1)
    causal = lambda: k_loc <= q_loc
    strictly_upper = lambda: k_loc > q_loc

    def step(kt, mask, k_ref, v_ref, q_lo, m_scr, acc_scr):
        start = pl.multiple_of(kt * tq, tq)
        s = _dot(qa_scr[:, q_lo:], k_ref[:, pl.ds(start, tq)])
        if mask is not None:
            s = jnp.where(mask(), s, NEG)
        m_prev = m_scr[...]
        m_new = jnp.maximum(m_prev, jnp.max(s, axis=-1, keepdims=True))
        alpha = jnp.exp(m_prev - m_new)
        p = jnp.concatenate([jnp.exp(s[:, j * LANES:(j + 1) * LANES] - m_new) for j in range(tq // LANES)], axis=-1)
        acc_scr[...] = alpha * acc_scr[...] + _dot(p.astype(BF16), v_ref[pl.ds(start, tq), :])
        m_scr[...] = m_new

    sel = functools.partial(step, k_ref=ksa_ref, v_ref=vs_ref, q_lo=0, m_scr=ms_scr, acc_scr=as_scr)
    win = functools.partial(step, k_ref=kw_ref, v_ref=vw_ref, q_lo=LANES, m_scr=mw_scr, acc_scr=aw_scr)

    for m_scr, acc_scr in ((ms_scr, as_scr), (mw_scr, aw_scr)):
        m_scr[...] = jnp.full(m_scr.shape, NEG, F32)
        acc_scr[...] = jnp.zeros(acc_scr.shape, F32)

    def sel_unrolled(kp, carry):
        for u in range(unroll):
            sel(unroll * kp + u, None)
        return carry

    lax.fori_loop(0, i // unroll, sel_unrolled, 0)
    base = (i // unroll) * unroll
    rem = i - base
    group = unroll // 2
    while group >= 1:
        first = base + (rem & ~(2 * group - 1))

        @pl.when((rem & group) != 0)
        def _(first=first, group=group):
            for u in range(group):
                sel(first + u, None)

        group //= 2

    @pl.when(i >= 2)
    def _():
        win(i - 2, strictly_upper)
        sel(i, causal)
        win(i - 1, None)
        win(i, causal)

    @pl.when(i < 2)
    def _():
        sel(i, causal)

        @pl.when(i == 1)
        def _():
            win(0, None)

        win(i, causal)

    acc_s = as_scr[...]
    o_sel = acc_s[:, :HEAD_DIM] / acc_s[:, HEAD_DIM:HEAD_DIM + 1]
    acc_w = aw_scr[...]
    o_win = acc_w[:, :HEAD_DIM] / acc_w[:, HEAD_DIM:HEAD_DIM + 1]

    gates = jax.nn.sigmoid(ng_ref[...].astype(F32))
    oc = oc_ref[...].astype(F32)
    lane = lax.broadcasted_iota(I32, gates.shape, 1)
    for h in range(HG):
        c0 = (g * HG + h) * 3
        gc = jnp.sum(jnp.where(lane == c0, gates, 0.0), axis=-1, keepdims=True)
        gs = jnp.sum(jnp.where(lane == c0 + 1, gates, 0.0), axis=-1, keepdims=True)
        gw = jnp.sum(jnp.where(lane == c0 + 2, gates, 0.0), axis=-1, keepdims=True)
        rows = slice(h * tq, (h + 1) * tq)
        cols = slice(h * HEAD_DIM, (h + 1) * HEAD_DIM)
        o_ref[:, cols] = (gc * oc[:, cols] + gs * o_sel[rows, :] + gw * o_win[rows, :]).astype(BF16)


def nsa_attention(proj, selbias, ksa_t, vs_aug, kw_t, vw_aug, oc, B, S, q_col, ng_col, tq=ATT_TILE, unroll=8):
    G = N_KV_GROUPS
    nt = S // tq
    gw = HEADS_PER_GROUP * HEAD_DIM
    assert WINDOW == 2 * tq and tq & (tq - 1) == 0 and unroll & (unroll - 1) == 0
    R = HEADS_PER_GROUP * tq
    per_group = lambda rows, cols: pl.BlockSpec((None, None, rows, cols), lambda b, g, i: (b, g, 0, 0))
    return pl.pallas_call(
        functools.partial(_nsa_kernel, tq=tq, unroll=unroll),
        grid=(B, G, nt),
        in_specs=[pl.BlockSpec((tq, gw), lambda b, g, i: (b * nt + i, q_col // gw + g)),
                  pl.BlockSpec((None, None, tq, LANES), lambda b, g, i: (b, g, i, 0)),
                  per_group(LANES + HEAD_DIM, S), per_group(S, LANES), per_group(HEAD_DIM, S), per_group(S, LANES),
                  pl.BlockSpec((tq, gw), lambda b, g, i: (b * nt + i, g)),
                  pl.BlockSpec((tq, LANES), lambda b, g, i: (b * nt + i, ng_col // LANES))],
        out_specs=pl.BlockSpec((tq, gw), lambda b, g, i: (b * nt + i, g)),
        out_shape=jax.ShapeDtypeStruct((B * S, G * gw), BF16),
        scratch_shapes=[pltpu.VMEM((R, LANES + HEAD_DIM), BF16),
                        pltpu.VMEM((R, LANES), F32), pltpu.VMEM((R, LANES), F32),
                        pltpu.VMEM((R, LANES), F32), pltpu.VMEM((R, LANES), F32)],
        compiler_params=_params("arbitrary", "arbitrary", "arbitrary"),
        name="nsa_attn",
    )(proj, selbias, ksa_t, vs_aug, kw_t, vw_aug, oc, proj)


def _route(logits, rbias_col):
    tm, E = logits.shape
    gsz = E // N_GROUPS
    lt = logits.T
    scores = jax.nn.sigmoid(lt)
    sfc = scores + rbias_col
    row_f = lax.broadcasted_iota(I32, (E, tm), 0).astype(F32)
    gscore = []
    for g in range(N_GROUPS):
        v = sfc[g * gsz:(g + 1) * gsz]
        m1 = jnp.max(v, axis=0, keepdims=True)
        cnt = jnp.sum(jnp.where(v == m1, 1.0, 0.0), axis=0, keepdims=True)
        m2 = jnp.max(jnp.where(v < m1, v, -jnp.inf), axis=0, keepdims=True)
        gscore.append(m1 + jnp.where(cnt >= 2.0, m1, m2))
    masked = []
    for g in range(N_GROUPS):
        rank = jnp.zeros((1, tm), F32)
        for o in range(N_GROUPS):
            if o == g:
                continue
            ahead = (gscore[o] > gscore[g]) | ((gscore[o] == gscore[g]) & (o < g))
            rank = rank + jnp.where(ahead, 1.0, 0.0)
        masked.append(jnp.where(rank < float(TOPK_GROUPS), sfc[g * gsz:(g + 1) * gsz], NEG))
    work = jnp.concatenate(masked, axis=0)
    out_row = lax.broadcasted_iota(I32, (LANES, tm), 0)
    idx_out = jnp.zeros((LANES, tm), F32)
    w_out = jnp.zeros((LANES, tm), F32)
    wsum = jnp.zeros((1, tm), F32)
    for r in range(TOP_K):
        top = jnp.max(work, axis=0, keepdims=True)
        first = jnp.min(jnp.where(work == top, row_f, float(E)), axis=0, keepdims=True)
        pick = row_f == first
        w = jnp.sum(jnp.where(pick, scores, 0.0), axis=0, keepdims=True)
        idx_out = jnp.where(out_row == r, first, idx_out)
        w_out = jnp.where(out_row == r, w, w_out)
        wsum = wsum + w
        work = jnp.where(pick, -jnp.inf, work)
    return idx_out.T.astype(I32), (w_out / wsum * ROUTED_SCALE).T


def _out_proj_kernel(o_ref, za_ref, mb_ref, x_ref, g1_ref, n2_ref, sc_ref, sh_ref, wn_ref, wo_ref,
                     wrh_ref, wrl_ref, rb_ref, x1_ref, h2_ref, idx_ref, wt_ref):
    yb = _dot(o_ref[...], wn_ref[...])
    z = za_ref[...].astype(F32) + jax.nn.sigmoid(mb_ref[...].astype(F32)) * yb
    mix = _dot(z.astype(BF16), wo_ref[...])
    x1 = x_ref[...] + g1_ref[...] * mix
    x1_ref[...] = x1
    h2 = _modulated_rmsnorm(x1, n2_ref[...], sc_ref[...], sh_ref[...])
    h2_ref[...] = h2
    hi, lo = _split_bf16(h2)
    logits = _dot(hi, wrh_ref[...]) + (_dot(lo, wrh_ref[...]) + _dot(hi, wrl_ref[...]))
    idx, wts = _route(logits, rb_ref[...])
    idx_ref[...] = idx
    wt_ref[...] = wts


def out_proj(o, za, proj, x2, g1, norm_g, sc, sh, w_nsa_out, w_out, w_router, router_bias, seq, mb_col, tm=256):
    T, D = x2.shape
    E = w_router.shape[1]
    per_b = seq // tm
    wr_hi, wr_lo = _split_bf16(w_router)
    tok = lambda i: (i, 0)
    const = lambda i: (0, 0)
    per_batch = pl.BlockSpec((None, 1, D), lambda i: (i // per_b, 0, 0))
    return pl.pallas_call(
        _out_proj_kernel,
        grid=(T // tm,),
        in_specs=[pl.BlockSpec((tm, D), tok), pl.BlockSpec((tm, D), tok),
                  pl.BlockSpec((tm, D), lambda i: (i, mb_col // D)),
                  pl.BlockSpec((tm, D), tok), per_batch,
                  pl.BlockSpec((1, D), const), per_batch, per_batch,
                  pl.BlockSpec((D, D), const), pl.BlockSpec((D, D), const),
                  pl.BlockSpec((D, E), const), pl.BlockSpec((D, E), const), pl.BlockSpec((E, 1), const)],
        out_specs=[pl.BlockSpec((tm, D), tok), pl.BlockSpec((tm, D), tok),
                   pl.BlockSpec((tm, LANES), tok), pl.BlockSpec((tm, LANES), tok)],
        out_shape=[jax.ShapeDtypeStruct((T, D), F32), jax.ShapeDtypeStruct((T, D), F32),
                   jax.ShapeDtypeStruct((T, LANES), I32), jax.ShapeDtypeStruct((T, LANES), F32)],
        compiler_params=_params("arbitrary"),
        name="out_proj",
    )(o, za, proj, x2, g1[:, None, :], norm_g.reshape(1, D), sc[:, None, :], sh[:, None, :],
      w_nsa_out.astype(BF16), w_out.astype(BF16), wr_hi, wr_lo, router_bias.reshape(E, 1))


def _experts_kernel(ex_ref, ni_ref, gidx_hbm, sidx_hbm, h_hbm, wg0, wu0, wd0, wg1, wu1, wd1, y_hbm,
                    g_idx, s_idx, x_buf, y_buf, g_sem, s_sem, row_sem, out_sem, *, bm, dump_row):
    n = pl.program_id(0)
    n_items = ni_ref[0]
    w0 = 2 * n
    w1 = w0 + 1

    def idx_copy(src_hbm, w, smem, sem, s):
        return pltpu.make_async_copy(src_hbm.at[w], smem.at[s], sem.at[s])

    gather_idx = functools.partial(idx_copy, gidx_hbm, smem=g_idx, sem=g_sem)
    scatter_idx = functools.partial(idx_copy, sidx_hbm, smem=s_idx, sem=s_sem)

    def issue_gather(s):
        for r in range(bm):
            pltpu.make_async_copy(h_hbm.at[pl.ds(g_idx[s, r], 1), :], x_buf.at[s, pl.ds(r, 1), :],
                                  row_sem.at[s]).start(priority=r % 2)

    def issue_scatter(s):
        for r in range(bm):
            pltpu.make_async_copy(y_buf.at[s, pl.ds(r, 1), :], y_hbm.at[pl.ds(s_idx[s, r], 1), :],
                                  out_sem.at[s]).start(priority=r % 2)

    def wait_gather(s):
        pltpu.make_async_copy(h_hbm.at[pl.ds(0, bm), :], x_buf.at[s], row_sem.at[s]).wait()

    def wait_scatter(s):
        pltpu.make_async_copy(y_buf.at[s], y_hbm.at[pl.ds(0, bm), :], out_sem.at[s]).wait()

    def expert(s, wg, wu, wd):
        x = x_buf[s].astype(BF16)
        gate = _dot(x, wg[...].astype(BF16))
        up = _dot(x, wu[...].astype(BF16))
        hid = gate * jax.nn.sigmoid(gate) * up
        y_buf[s] = _dot(hid.astype(BF16), wd[...].astype(BF16))

    @pl.when(n == 0)
    def _():
        y_buf[...] = jnp.zeros(y_buf.shape, F32)
        for s in range(2):
            pltpu.make_async_copy(y_buf.at[s], y_hbm.at[pl.ds(dump_row + s * bm, bm), :], out_sem.at[s]).start()
        gather_idx(w=0, s=0).start()
        gather_idx(w=1, s=1).start()
        gather_idx(w=0, s=0).wait()
        issue_gather(0)

    @pl.when(w0 < n_items)
    def _():
        gather_idx(w=w0 + 2, s=0).start()
        scatter_idx(w=w0, s=0).start()
        scatter_idx(w=w1, s=1).start()
        gather_idx(w=w1, s=1).wait()
        issue_gather(1)
        gather_idx(w=w1 + 2, s=1).start()
        wait_gather(0)
        wait_scatter(0)
        expert(0, wg0, wu0, wd0)
        scatter_idx(w=w0, s=0).wait()
        issue_scatter(0)
        gather_idx(w=w0 + 2, s=0).wait()
        issue_gather(0)
        wait_gather(1)
        wait_scatter(1)
        expert(1, wg1, wu1, wd1)
        scatter_idx(w=w1, s=1).wait()
        issue_scatter(1)

        @pl.when(w0 + 2 >= n_items)
        def _():
            wait_gather(0)
            gather_idx(w=w1 + 2, s=1).wait()
            wait_scatter(0)
            wait_scatter(1)


def routed_experts(h2, plan, w_gate, w_up, w_down, n_assign, bm=EXPERT_BLOCK):
    ex, n_items, gidx, sidx = plan
    T, D = h2.shape
    E, _, DE = w_gate.shape
    n_work = ex.shape[0] - 2
    even = lambda n, ex, ni: (ex[2 * n], 0, 0)
    odd = lambda n, ex, ni: (ex[2 * n + 1], 0, 0)
    grid_spec = pltpu.PrefetchScalarGridSpec(
        num_scalar_prefetch=2,
        grid=(n_work // 2,),
        in_specs=[pl.BlockSpec(memory_space=pl.ANY), pl.BlockSpec(memory_space=pl.ANY),
                  pl.BlockSpec(memory_space=pl.ANY),
                  pl.BlockSpec((None, D, DE), even), pl.BlockSpec((None, D, DE), even), pl.BlockSpec((None, DE, D), even),
                  pl.BlockSpec((None, D, DE), odd), pl.BlockSpec((None, D, DE), odd), pl.BlockSpec((None, DE, D), odd)],
        out_specs=pl.BlockSpec(memory_space=pl.ANY),
        scratch_shapes=[pltpu.SMEM((2, bm), I32), pltpu.SMEM((2, bm), I32),
                        pltpu.VMEM((2, bm, D), F32), pltpu.VMEM((2, bm, D), F32),
                        pltpu.SemaphoreType.DMA((2,)), pltpu.SemaphoreType.DMA((2,)),
                        pltpu.SemaphoreType.DMA((2,)), pltpu.SemaphoreType.DMA((2,))],
    )
    return pl.pallas_call(
        functools.partial(_experts_kernel, bm=bm, dump_row=n_assign),
        grid_spec=grid_spec,
        out_shape=jax.ShapeDtypeStruct((n_assign + 2 * bm, D), F32),
        compiler_params=_params("arbitrary"),
        name="experts",
    )(ex, n_items, gidx, sidx, h2, w_gate, w_up, w_down, w_gate, w_up, w_down)


def _combine_kernel(*refs):
    y8_refs = refs[:TOP_K]
    h2_ref, x1_ref, wt_ref, g2_ref, fg_ref, wsg_ref, wsu_ref, wsd_ref, o_ref = refs[TOP_K:]
    h2 = h2_ref[...].astype(BF16)
    gate = _dot(h2, wsg_ref[...])
    up = _dot(h2, wsu_ref[...])
    y = _dot((gate * jax.nn.sigmoid(gate) * up).astype(BF16), wsd_ref[...])
    wt = wt_ref[...]
    for k in range(TOP_K):
        y = y + wt[:, k:k + 1] * y8_refs[k][...]
    x2 = x1_ref[...] + g2_ref[...] * y
    o_ref[...] = x2 * lax.rsqrt(jnp.mean(x2 * x2, axis=-1, keepdims=True) + EPS) * fg_ref[...]


def combine(y8, h2, x1, wts, g2, final_g, w_sh_gate, w_sh_up, w_sh_down, seq, tj=COMBINE_TILE):
    T, D = x1.shape
    DS = w_sh_gate.shape[1]
    per_b = seq // tj
    nt = T // tj
    tok = lambda i: (i, 0)
    const = lambda i: (0, 0)
    slot = lambda k: pl.BlockSpec((tj, D), lambda i: (k * nt + i, 0))
    return pl.pallas_call(
        _combine_kernel,
        grid=(nt,),
        in_specs=[slot(k) for k in range(TOP_K)] + [
                  pl.BlockSpec((tj, D), tok), pl.BlockSpec((tj, D), tok), pl.BlockSpec((tj, LANES), tok),
                  pl.BlockSpec((None, 1, D), lambda i: (i // per_b, 0, 0)),
                  pl.BlockSpec((1, D), const),
                  pl.BlockSpec((D, DS), const), pl.BlockSpec((D, DS), const), pl.BlockSpec((DS, D), const)],
        out_specs=pl.BlockSpec((tj, D), tok),
        out_shape=jax.ShapeDtypeStruct((T, D), F32),
        compiler_params=_params("arbitrary"),
        name="combine",
    )(*([y8] * TOP_K), h2, x1, wts, g2[:, None, :], final_g.reshape(1, D),
      w_sh_gate.astype(BF16), w_sh_up.astype(BF16), w_sh_down.astype(BF16))


def _dispatch_plan(idx, T, bm):
    A = T * TOP_K
    assert A % bm == 0
    nblk = A // bm
    E = N_EXPERTS
    kbits = TOP_K.bit_length() - 1
    sorted_e, order = lax.sort((idx.reshape(A), jnp.arange(A, dtype=I32)), num_keys=1)
    bounds = jnp.searchsorted(sorted_e, jnp.arange(E + 1, dtype=I32), side='left').astype(I32)
    start, end = bounds[:-1], bounds[1:]
    first_blk = start // bm
    n_it = jnp.where(end > start, (end - 1) // bm - first_blk + 1, 0)
    it_end = jnp.cumsum(n_it)
    it_start = it_end - n_it
    n_items = it_end[-1]
    n_work = nblk + E
    w = jnp.arange(n_work + 2, dtype=I32)
    real = w < n_items
    wc = jnp.minimum(w, n_items - 1)
    ex = jnp.searchsorted(it_end, wc, side='right').astype(I32)
    blk = first_blk[ex] + wc - it_start[ex]
    lo = jnp.maximum(start[ex], blk * bm) - blk * bm
    hi = jnp.where(real, jnp.minimum(end[ex], (blk + 1) * bm) - blk * bm, lo)
    rows = order.reshape(nblk, bm)[blk]
    r = jnp.arange(bm, dtype=I32)[None, :]
    mine = (r >= lo[:, None]) & (r < hi[:, None])
    gidx = rows >> kbits
    sidx = jnp.where(mine, (rows & (TOP_K - 1)) * T + gidx, A + (w[:, None] % 2) * bm + r)
    return ex, n_items.astype(I32).reshape(1), gidx.astype(I32), sidx.astype(I32)


def _layer(x, c, w_ada, b_ada, norm1_g, w_in, b_in, conv_w, conv_b, conv_ln_g, conv_ln_b, w_conv_out, b_conv_out,
           cmp_pe_k, cmp_w1_k, cmp_w2_k, cmp_pe_v, cmp_w1_v, cmp_w2_v, w_nsa_out, w_out, norm2_g, w_router,
           router_bias, w_sh_gate, w_sh_up, w_sh_down, w_gate, w_up, w_down, final_g):
    B, S, D = x.shape
    T = B * S
    G, dh = N_KV_GROUPS, HEAD_DIM
    n_q = N_HEADS * dh
    n_kv = 3 * 2 * G * dh
    n_ng = 3 * N_HEADS
    mod = ada_modulation(c, w_ada, b_ada)
    sh1, sc1, g1, sh2, sc2, g2 = jnp.split(mod, 6, axis=-1)

    c_q = 2 * D_CONV
    c_kv = c_q + n_q
    c_ng = c_kv + n_kv
    c_mg = c_ng + n_ng
    pad = LANES - n_ng
    w_r = jnp.concatenate([w_in[:, :c_kv], w_in[:, c_mg:], w_in[:, c_kv:c_mg], jnp.zeros((D, pad), F32)], axis=1)
    b_r = jnp.concatenate([b_in[:c_kv], b_in[c_mg:], b_in[c_kv:c_mg], jnp.zeros((pad,), F32)])
    col_merge = c_kv
    col_kv = col_merge + 2 * D
    col_ng = col_kv + n_kv
    x2 = x.reshape(T, D)
    proj = in_proj(x2, norm1_g, sc1, sh1, w_r.astype(BF16), b_r, S)

    za = conformer_conv(proj, B, S, D, col_merge, conv_w, conv_b, conv_ln_g, conv_ln_b, w_conv_out, b_conv_out)

    kv = proj[:, col_kv:col_kv + n_kv].reshape(B, S, 3, 2, G, dh)
    NC = S // CMP_STRIDE
    zc = kv[:, :, 0].reshape(B, NC, CMP_STRIDE, 2, G, dh).transpose(0, 3, 4, 1, 2, 5).reshape(B, 2, G, NC, CMP_STRIDE * dh)
    kvc = compress_kv(zc, jnp.stack([cmp_pe_k, cmp_pe_v]), jnp.stack([cmp_w1_k, cmp_w1_v]),
                      jnp.stack([cmp_w2_k, cmp_w2_v]))

    cs = CMP_STRIDE * jnp.arange(NC)[:, None]
    ss = SEL_BLOCK * jnp.arange(LANES)[None, :]
    overlap = ((cs <= ss + SEL_BLOCK - 1) & (cs + CMP_BLOCK - 1 >= ss) & (jnp.arange(NC)[:, None] < NC - 1)
               & (jnp.arange(LANES)[None, :] < S // SEL_BLOCK)).astype(BF16)
    oc, selbias = cmp_attention(proj, kvc[:, 0].transpose(0, 1, 3, 2), kvc[:, 1], overlap, B, S, c_q)

    keys_t = lambda br: kv[:, :, br, 0].transpose(0, 2, 3, 1)
    ones_col = jnp.concatenate([jnp.ones((B, G, S, 1), BF16), jnp.zeros((B, G, S, LANES - dh - 1), BF16)], axis=-1)
    vals_aug = lambda br: jnp.concatenate([kv[:, :, br, 1].transpose(0, 2, 1, 3), ones_col], axis=-1)
    onehot_t = (jnp.arange(LANES)[:, None] == jnp.arange(S)[None, :] // SEL_BLOCK).astype(BF16)
    ksa_t = jnp.concatenate([jnp.broadcast_to(onehot_t, (B, G, LANES, S)), keys_t(1)], axis=2)
    o = nsa_attention(proj, selbias, ksa_t, vals_aug(1), keys_t(2), vals_aug(2), oc, B, S, c_q, col_ng)

    x1, h2, idx, wts = out_proj(o, za, proj, x2, g1, norm2_g, sc2, sh2, w_nsa_out, w_out, w_router, router_bias,
                                S, col_merge + D)

    plan = _dispatch_plan(idx[:, :TOP_K], T, EXPERT_BLOCK)
    y8 = routed_experts(h2, plan, w_gate, w_up, w_down, T * TOP_K)
    out = combine(y8, h2, x1, wts, g2, final_g, w_sh_gate, w_sh_up, w_sh_down, S)
    return out.reshape(B, S, D)


def kernel(x, c, w_ada, b_ada, norm1_g, w_in, b_in, conv_w, conv_b, conv_ln_g, conv_ln_b, w_conv_out, b_conv_out, cmp_pe_k, cmp_w1_k, cmp_w2_k, cmp_pe_v, cmp_w1_v, cmp_w2_v, w_nsa_out, w_out, norm2_g, w_router, router_bias, w_sh_gate, w_sh_up, w_sh_down, w_gate, w_up, w_down, final_g):
    assert w_ada.shape[0] == 1, "single-layer block"
    return _layer(x, c, w_ada[0], b_ada[0], norm1_g[0], w_in[0], b_in[0], conv_w[0], conv_b[0], conv_ln_g[0],
                  conv_ln_b[0], w_conv_out[0], b_conv_out[0], cmp_pe_k[0], cmp_w1_k[0], cmp_w2_k[0], cmp_pe_v[0],
                  cmp_w1_v[0], cmp_w2_v[0], w_nsa_out[0], w_out[0], norm2_g[0], w_router[0], router_bias[0],
                  w_sh_gate[0], w_sh_up[0], w_sh_down[0], w_gate[0], w_up[0], w_down[0], final_g)
```

```python
import functools

import jax
import jax.numpy as jnp
from jax import lax
from jax.experimental import pallas as pl
from jax.experimental.pallas import tpu as pltpu

BF16 = jnp.bfloat16
F32 = jnp.float32
I32 = jnp.int32

D_CONV = 512
CONV_K = 31
N_HEADS = 16
N_KV_GROUPS = 4
HEADS_PER_GROUP = 4
HEAD_DIM = 64
CMP_BLOCK = 32
CMP_STRIDE = 16
SEL_BLOCK = 64
SEL_TOPK = 16
WINDOW = 512
FORCED_SCORE = 1e4
N_EXPERTS = 256
TOP_K = 8
N_GROUPS = 8
TOPK_GROUPS = 4
ROUTED_SCALE = 2.5
EPS = 1e-6
NEG = -1e30
MASK_BIAS = -1e9
LANES = 128
HALO = 32
ATT_TILE = 256
EXPERT_BLOCK = 256
COMBINE_TILE = 256
VMEM_LIMIT = 56 * 1024 * 1024


def _params(*sem):
    return pltpu.CompilerParams(dimension_semantics=sem, vmem_limit_bytes=VMEM_LIMIT)


def _dot(a, b):
    return jnp.dot(a, b, preferred_element_type=F32)


def _split_bf16(x):
    hi = x.astype(BF16)
    lo = (x - hi.astype(F32)).astype(BF16)
    return hi, lo


def _ada_kernel(c_ref, w_ref, b_ref, o_ref):
    c = c_ref[...]
    a = c * jax.nn.sigmoid(c)
    o_ref[...] = _dot(a.astype(BF16), w_ref[...].astype(BF16)) + b_ref[...]


def ada_modulation(c, w_ada, b_ada):
    B, D = c.shape
    N = w_ada.shape[1]
    rows = 8
    c8 = jnp.zeros((rows, D), F32).at[:B].set(c)
    tn = 1024
    out = pl.pallas_call(
        _ada_kernel,
        grid=(N // tn,),
        in_specs=[pl.BlockSpec((rows, D), lambda j: (0, 0)),
                  pl.BlockSpec((D, tn), lambda j: (0, j)),
                  pl.BlockSpec((1, tn), lambda j: (0, j))],
        out_specs=pl.BlockSpec((rows, tn), lambda j: (0, j)),
        out_shape=jax.ShapeDtypeStruct((rows, N), F32),
        compiler_params=_params("arbitrary"),
        name="ada",
    )(c8, w_ada, b_ada.reshape(1, N))
    return out[:B]


def _modulated_rmsnorm(x, g, sc, sh):
    y = x * lax.rsqrt(jnp.mean(x * x, axis=-1, keepdims=True) + EPS)
    return (y * g) * (1.0 + sc) + sh


def _in_proj_kernel(x_ref, g_ref, sc_ref, sh_ref, w_ref, b_ref, o_ref, h_scr):
    @pl.when(pl.program_id(1) == 0)
    def _():
        h = _modulated_rmsnorm(x_ref[...], g_ref[...], sc_ref[...], sh_ref[...])
        h_scr[...] = h.astype(BF16)

    o_ref[...] = (_dot(h_scr[...], w_ref[...]) + b_ref[...]).astype(BF16)


def in_proj(x2, norm_g, sc, sh, w, b, seq, tm=1024, tn=1152):
    T, D = x2.shape
    NP = w.shape[1]
    per_b = seq // tm
    return pl.pallas_call(
        _in_proj_kernel,
        grid=(T // tm, NP // tn),
        in_specs=[pl.BlockSpec((tm, D), lambda i, j: (i, 0)),
                  pl.BlockSpec((1, D), lambda i, j: (0, 0)),
                  pl.BlockSpec((None, 1, D), lambda i, j: (i // per_b, 0, 0)),
                  pl.BlockSpec((None, 1, D), lambda i, j: (i // per_b, 0, 0)),
                  pl.BlockSpec((D, tn), lambda i, j: (0, j)),
                  pl.BlockSpec((1, tn), lambda i, j: (0, j))],
        out_specs=pl.BlockSpec((tm, tn), lambda i, j: (i, j)),
        out_shape=jax.ShapeDtypeStruct((T, NP), BF16),
        scratch_shapes=[pltpu.VMEM((tm, D), BF16)],
        compiler_params=_params("arbitrary", "arbitrary"),
        name="in_proj",
    )(x2, norm_g.reshape(1, D), sc[:, None, :], sh[:, None, :], w, b.reshape(1, NP))


def _conv_kernel(a_ref, gt_ref, pa_ref, pg_ref, mg_ref, cw_ref, cb_ref, lg_ref, lb_ref, wo_ref, bo_ref,
                 o_ref, u_scr, c_scr, *, ts, chunk):
    i = pl.program_id(1)
    a = a_ref[...].astype(F32)
    u_scr[HALO:, :] = a * jax.nn.sigmoid(gt_ref[...].astype(F32))
    pa = pa_ref[...].astype(F32)
    prev = pa * jax.nn.sigmoid(pg_ref[...].astype(F32))
    u_scr[:HALO, :] = jnp.where(i > 0, prev, 0.0)
    off = HALO - (CONV_K - 1)
    for r0 in range(0, ts, chunk):
        acc = jnp.broadcast_to(cb_ref[...], (chunk, D_CONV))
        for k in range(CONV_K):
            acc = acc + cw_ref[k:k + 1, :] * u_scr[r0 + off + k:r0 + off + k + chunk, :]
        c_scr[r0:r0 + chunk, :] = acc
    v = c_scr[...]
    mu = jnp.mean(v, axis=-1, keepdims=True)
    var = jnp.mean(jnp.square(v - mu), axis=-1, keepdims=True)
    y = (v - mu) * lax.rsqrt(var + EPS) * lg_ref[...] + lb_ref[...]
    y = y * jax.nn.sigmoid(y)
    ya = _dot(y.astype(BF16), wo_ref[...]) + bo_ref[...]
    o_ref[...] = (jax.nn.sigmoid(mg_ref[...].astype(F32)) * ya).astype(BF16)


def conformer_conv(proj, B, S, D, merge_col, conv_w, conv_b, ln_g, ln_b, w_o, b_o, ts=512, chunk=64):
    T = B * S
    nt = S // ts
    hb = ts // HALO
    prev_idx = lambda b, i: (jnp.maximum((b * nt + i) * hb - 1, 0), 0)
    prev_idx_g = lambda b, i: (jnp.maximum((b * nt + i) * hb - 1, 0), 1)
    row = lambda v: v.reshape(1, -1)
    return pl.pallas_call(
        functools.partial(_conv_kernel, ts=ts, chunk=chunk),
        grid=(B, nt),
        in_specs=[pl.BlockSpec((ts, D_CONV), lambda b, i: (b * nt + i, 0)),
                  pl.BlockSpec((ts, D_CONV), lambda b, i: (b * nt + i, 1)),
                  pl.BlockSpec((HALO, D_CONV), prev_idx),
                  pl.BlockSpec((HALO, D_CONV), prev_idx_g),
                  pl.BlockSpec((ts, D), lambda b, i: (b * nt + i, merge_col // D)),
                  pl.BlockSpec((CONV_K, D_CONV), lambda b, i: (0, 0)),
                  pl.BlockSpec((1, D_CONV), lambda b, i: (0, 0)),
                  pl.BlockSpec((1, D_CONV), lambda b, i: (0, 0)),
                  pl.BlockSpec((1, D_CONV), lambda b, i: (0, 0)),
                  pl.BlockSpec((D_CONV, D), lambda b, i: (0, 0)),
                  pl.BlockSpec((1, D), lambda b, i: (0, 0))],
        out_specs=pl.BlockSpec((ts, D), lambda b, i: (b * nt + i, 0)),
        out_shape=jax.ShapeDtypeStruct((T, D), BF16),
        scratch_shapes=[pltpu.VMEM((ts + HALO, D_CONV), F32), pltpu.VMEM((ts, D_CONV), F32)],
        compiler_params=_params("arbitrary", "arbitrary"),
        name="conv",
    )(proj, proj, proj, proj, proj, conv_w, row(conv_b), row(ln_g), row(ln_b), w_o.astype(BF16), row(b_o))


def _compress_kernel(z_ref, pe_ref, w1_ref, w2_ref, o_ref, *, nc):
    half = w1_ref.shape[0] // 2
    z = z_ref[...]
    first = _dot(z, w1_ref[:half, :])
    second = _dot(z, w1_ref[half:, :])
    bias = _dot(pe_ref[...], w1_ref[...])[0:1, :]
    hid = first + pltpu.roll(second, nc - 1, 0) + bias
    act = jax.nn.gelu(hid)
    out = _dot(act.astype(BF16), w2_ref[...])
    rows = lax.broadcasted_iota(I32, out.shape, 0)
    o_ref[...] = jnp.where(rows < nc - 1, out, 0.0).astype(BF16)


def compress_kv(zc, pe, w1, w2):
    B, _, G, NC, W = zc.shape
    H = w1.shape[-1]
    pe8 = jnp.broadcast_to(pe.reshape(2, 1, 2 * W), (2, 8, 2 * W)).astype(BF16)
    return pl.pallas_call(
        functools.partial(_compress_kernel, nc=NC),
        grid=(B, 2, G),
        in_specs=[pl.BlockSpec((None, None, None, NC, W), lambda b, s, g: (b, s, g, 0, 0)),
                  pl.BlockSpec((None, 8, 2 * W), lambda b, s, g: (s, 0, 0)),
                  pl.BlockSpec((None, 2 * W, H), lambda b, s, g: (s, 0, 0)),
                  pl.BlockSpec((None, H, HEAD_DIM), lambda b, s, g: (s, 0, 0))],
        out_specs=pl.BlockSpec((None, None, None, NC, HEAD_DIM), lambda b, s, g: (b, s, g, 0, 0)),
        out_shape=jax.ShapeDtypeStruct((B, 2, G, NC, HEAD_DIM), BF16),
        compiler_params=_params("arbitrary", "arbitrary", "arbitrary"),
        name="compress",
    )(zc, pe8, w1.astype(BF16), w2.astype(BF16))


def _cmp_attn_kernel(q_ref, kct_ref, vc_ref, ov_ref, oc_ref, sb_ref, qa_scr, imp_scr, *, tq, ns):
    i = pl.program_id(2)
    t0 = i * tq
    HG = HEADS_PER_GROUP
    R = HG * tq
    nc = kct_ref.shape[1]
    q = q_ref[...] * (HEAD_DIM ** -0.5)
    for h in range(HG):
        qa_scr[h * tq:(h + 1) * tq, :] = q[:, h * HEAD_DIM:(h + 1) * HEAD_DIM]

    def attend(width):
        t_ids = t0 + (lax.broadcasted_iota(I32, (R, width), 0) & (tq - 1))
        n_ids = lax.broadcasted_iota(I32, (R, width), 1)
        valid = (CMP_STRIDE * n_ids + (CMP_BLOCK - 1)) <= t_ids
        s = jnp.where(valid, _dot(qa_scr[...], kct_ref[:, :width]), NEG)
        m = jnp.max(s, axis=-1, keepdims=True)
        e = jnp.where(valid, jnp.exp(s - m), 0.0)
        den = jnp.sum(e, axis=-1, keepdims=True)
        p = e * jnp.where(den > 0.0, 1.0 / den, 0.0)
        o = _dot(p.astype(BF16), vc_ref[:width, :])
        for h in range(HG):
            oc_ref[:, h * HEAD_DIM:(h + 1) * HEAD_DIM] = o[h * tq:(h + 1) * tq, :].astype(BF16)
        psum = p[:tq]
        for h in range(1, HG):
            psum = psum + p[h * tq:(h + 1) * tq]
        ps_hi, ps_lo = _split_bf16(psum)
        imp_scr[...] = _dot(ps_hi, ov_ref[:width, :]) + _dot(ps_lo, ov_ref[:width, :])

    visible = (t0 + tq - (CMP_BLOCK - CMP_STRIDE)) // CMP_STRIDE
    n_tiles = jnp.clip((visible + LANES - 1) // LANES, 1, nc // LANES)
    for k in range(1, nc // LANES + 1):
        pl.when(n_tiles == k)(functools.partial(attend, k * LANES))

    imp = imp_scr[...]
    lane = lax.broadcasted_iota(I32, (tq, LANES), 1)
    jq = (t0 + lax.broadcasted_iota(I32, (tq, LANES), 0)) >> (SEL_BLOCK.bit_length() - 1)
    forced = (lane == 0) | (lane == jq) | (lane == jq - 1)
    imp = jnp.where(forced, FORCED_SCORE, jnp.where(lane <= jq, imp, -1.0))
    work = jnp.where(lane < ns, imp, -jnp.inf).T
    blk_f = lax.broadcasted_iota(I32, (LANES, tq), 0).astype(F32)
    sel = jnp.zeros((LANES, tq), F32)
    for _ in range(SEL_TOPK):
        top = jnp.max(work, axis=0, keepdims=True)
        first = jnp.min(jnp.where(work == top, blk_f, float(LANES)), axis=0, keepdims=True)
        pick = blk_f == first
        sel = jnp.where(pick, 1.0, sel)
        work = jnp.where(pick, -jnp.inf, work)
    sb_ref[...] = jnp.where((sel.T > 0.5) & (lane <= jq), 0.0, MASK_BIAS).astype(BF16)


def cmp_attention(proj, kc_t, vc, overlap, B, S, q_col, tq=ATT_TILE):
    G = N_KV_GROUPS
    NC = vc.shape[2]
    nt = S // tq
    gw = HEADS_PER_GROUP * HEAD_DIM
    ns = S // SEL_BLOCK
    assert SEL_TOPK <= ns <= LANES and tq & (tq - 1) == 0 and NC % LANES == 0
    return pl.pallas_call(
        functools.partial(_cmp_attn_kernel, tq=tq, ns=ns),
        grid=(B, G, nt),
        in_specs=[pl.BlockSpec((tq, gw), lambda b, g, i: (b * nt + i, q_col // gw + g)),
                  pl.BlockSpec((None, None, HEAD_DIM, NC), lambda b, g, i: (b, g, 0, 0)),
                  pl.BlockSpec((None, None, NC, HEAD_DIM), lambda b, g, i: (b, g, 0, 0)),
                  pl.BlockSpec((NC, LANES), lambda b, g, i: (0, 0))],
        out_specs=[pl.BlockSpec((tq, gw), lambda b, g, i: (b * nt + i, g)),
                   pl.BlockSpec((None, None, tq, LANES), lambda b, g, i: (b, g, i, 0))],
        out_shape=[jax.ShapeDtypeStruct((B * S, G * gw), BF16),
                   jax.ShapeDtypeStruct((B, G, S, LANES), BF16)],
        scratch_shapes=[pltpu.VMEM((HEADS_PER_GROUP * tq, HEAD_DIM), BF16), pltpu.VMEM((tq, LANES), F32)],
        compiler_params=_params("arbitrary", "arbitrary", "arbitrary"),
        name="cmp_attn",
    )(proj, kc_t, vc, overlap)


def _nsa_kernel(q_ref, sb_ref, ksa_ref, vs_ref, kw_ref, vw_ref, oc_ref, ng_ref, o_ref,
                qa_scr, ms_scr, as_scr, mw_scr, aw_scr, *, tq, unroll):
    g = pl.program_id(1)
    i = pl.program_id(2)
    HG = HEADS_PER_GROUP
    R = HG * tq
    q = q_ref[...] * (HEAD_DIM ** -0.5)
    sb = sb_ref[...]
    for h in range(HG):
        qa_scr[h * tq:(h + 1) * tq, :LANES] = sb
        qa_scr[h * tq:(h + 1) * tq, LANES:] = q[:, h * HEAD_DIM:(h + 1) * HEAD_DIM]

    q_loc = lax.broadcasted_iota(I32, (R, tq), 0) & (tq - 1)
    k_loc = lax.broadcasted_iota(I32, (R, tq), 1)
    causal = lambda: k_loc <= q_loc
    strictly_upper = lambda: k_loc > q_loc

    def step(kt, mask, k_ref, v_ref, q_lo, m_scr, acc_scr):
        start = pl.multiple_of(kt * tq, tq)
        s = _dot(qa_scr[:, q_lo:], k_ref[:, pl.ds(start, tq)])
        if mask is not None:
            s = jnp.where(mask(), s, NEG)
        m_prev = m_scr[...]
        m_new = jnp.maximum(m_prev, jnp.max(s, axis=-1, keepdims=True))
        alpha = jnp.exp(m_prev - m_new)
        p = jnp.concatenate([jnp.exp(s[:, j * LANES:(j + 1) * LANES] - m_new) for j in range(tq // LANES)], axis=-1)
        acc_scr[...] = alpha * acc_scr[...] + _dot(p.astype(BF16), v_ref[pl.ds(start, tq), :])
        m_scr[...] = m_new

    sel = functools.partial(step, k_ref=ksa_ref, v_ref=vs_ref, q_lo=0, m_scr=ms_scr, acc_scr=as_scr)
    win = functools.partial(step, k_ref=kw_ref, v_ref=vw_ref, q_lo=LANES, m_scr=mw_scr, acc_scr=aw_scr)

    for m_scr, acc_scr in ((ms_scr, as_scr), (mw_scr, aw_scr)):
        m_scr[...] = jnp.full(m_scr.shape, NEG, F32)
        acc_scr[...] = jnp.zeros(acc_scr.shape, F32)

    def sel_unrolled(kp, carry):
        for u in range(unroll):
            sel(unroll * kp + u, None)
        return carry

    lax.fori_loop(0, i // unroll, sel_unrolled, 0)
    base = (i // unroll) * unroll
    rem = i - base
    group = unroll // 2
    while group >= 1:
        first = base + (rem & ~(2 * group - 1))

        @pl.when((rem & group) != 0)
        def _(first=first, group=group):
            for u in range(group):
                sel(first + u, None)

        group //= 2

    @pl.when(i >= 2)
    def _():
        win(i - 2, strictly_upper)
        sel(i, causal)
        win(i - 1, None)
        win(i, causal)

    @pl.when(i < 2)
    def _():
        sel(i, causal)

        @pl.when(i == 1)
        def _():
            win(0, None)

        win(i, causal)

    acc_s = as_scr[...]
    o_sel = acc_s[:, :HEAD_DIM] / acc_s[:, HEAD_DIM:HEAD_DIM + 1]
    acc_w = aw_scr[...]
    o_win = acc_w[:, :HEAD_DIM] / acc_w[:, HEAD_DIM:HEAD_DIM + 1]

    gates = jax.nn.sigmoid(ng_ref[...].astype(F32))
    oc = oc_ref[...].astype(F32)
    lane = lax.broadcasted_iota(I32, gates.shape, 1)
    for h in range(HG):
        c0 = (g * HG + h) * 3
        gc = jnp.sum(jnp.where(lane == c0, gates, 0.0), axis=-1, keepdims=True)
        gs = jnp.sum(jnp.where(lane == c0 + 1, gates, 0.0), axis=-1, keepdims=True)
        gw = jnp.sum(jnp.where(lane == c0 + 2, gates, 0.0), axis=-1, keepdims=True)
        rows = slice(h * tq, (h + 1) * tq)
        cols = slice(h * HEAD_DIM, (h + 1) * HEAD_DIM)
        o_ref[:, cols] = (gc * oc[:, cols] + gs * o_sel[rows, :] + gw * o_win[rows, :]).astype(BF16)


def nsa_attention(proj, selbias, ksa_t, vs_aug, kw_t, vw_aug, oc, B, S, q_col, ng_col, tq=ATT_TILE, unroll=8):
    G = N_KV_GROUPS
    nt = S // tq
    gw = HEADS_PER_GROUP * HEAD_DIM
    assert WINDOW == 2 * tq and tq & (tq - 1) == 0 and unroll & (unroll - 1) == 0
    R = HEADS_PER_GROUP * tq
    per_group = lambda rows, cols: pl.BlockSpec((None, None, rows, cols), lambda b, g, i: (b, g, 0, 0))
    return pl.pallas_call(
        functools.partial(_nsa_kernel, tq=tq, unroll=unroll),
        grid=(B, G, nt),
        in_specs=[pl.BlockSpec((tq, gw), lambda b, g, i: (b * nt + i, q_col // gw + g)),
                  pl.BlockSpec((None, None, tq, LANES), lambda b, g, i: (b, g, i, 0)),
                  per_group(LANES + HEAD_DIM, S), per_group(S, LANES), per_group(HEAD_DIM, S), per_group(S, LANES),
                  pl.BlockSpec((tq, gw), lambda b, g, i: (b * nt + i, g)),
                  pl.BlockSpec((tq, LANES), lambda b, g, i: (b * nt + i, ng_col // LANES))],
        out_specs=pl.BlockSpec((tq, gw), lambda b, g, i: (b * nt + i, g)),
        out_shape=jax.ShapeDtypeStruct((B * S, G * gw), BF16),
        scratch_shapes=[pltpu.VMEM((R, LANES + HEAD_DIM), BF16),
                        pltpu.VMEM((R, LANES), F32), pltpu.VMEM((R, LANES), F32),
                        pltpu.VMEM((R, LANES), F32), pltpu.VMEM((R, LANES), F32)],
        compiler_params=_params("arbitrary", "arbitrary", "arbitrary"),
        name="nsa_attn",
    )(proj, selbias, ksa_t, vs_aug, kw_t, vw_aug, oc, proj)


def _route(logits, rbias_col):
    tm, E = logits.shape
    gsz = E // N_GROUPS
    lt = logits.T
    scores = jax.nn.sigmoid(lt)
    sfc = scores + rbias_col
    row_f = lax.broadcasted_iota(I32, (E, tm), 0).astype(F32)
    gscore = []
    for g in range(N_GROUPS):
        v = sfc[g * gsz:(g + 1) * gsz]
        m1 = jnp.max(v, axis=0, keepdims=True)
        cnt = jnp.sum(jnp.where(v == m1, 1.0, 0.0), axis=0, keepdims=True)
        m2 = jnp.max(jnp.where(v < m1, v, -jnp.inf), axis=0, keepdims=True)
        gscore.append(m1 + jnp.where(cnt >= 2.0, m1, m2))
    masked = []
    for g in range(N_GROUPS):
        rank = jnp.zeros((1, tm), F32)
        for o in range(N_GROUPS):
            if o == g:
                continue
            ahead = (gscore[o] > gscore[g]) | ((gscore[o] == gscore[g]) & (o < g))
            rank = rank + jnp.where(ahead, 1.0, 0.0)
        masked.append(jnp.where(rank < float(TOPK_GROUPS), sfc[g * gsz:(g + 1) * gsz], NEG))
    work = jnp.concatenate(masked, axis=0)
    out_row = lax.broadcasted_iota(I32, (LANES, tm), 0)
    idx_out = jnp.zeros((LANES, tm), F32)
    w_out = jnp.zeros((LANES, tm), F32)
    wsum = jnp.zeros((1, tm), F32)
    for r in range(TOP_K):
        top = jnp.max(work, axis=0, keepdims=True)
        first = jnp.min(jnp.where(work == top, row_f, float(E)), axis=0, keepdims=True)
        pick = row_f == first
        w = jnp.sum(jnp.where(pick, scores, 0.0), axis=0, keepdims=True)
        idx_out = jnp.where(out_row == r, first, idx_out)
        w_out = jnp.where(out_row == r, w, w_out)
        wsum = wsum + w
        work = jnp.where(pick, -jnp.inf, work)
    return idx_out.T.astype(I32), (w_out / wsum * ROUTED_SCALE).T


def _out_proj_kernel(o_ref, za_ref, mb_ref, x_ref, g1_ref, n2_ref, sc_ref, sh_ref, wn_ref, wo_ref,
                     wrh_ref, wrl_ref, rb_ref, x1_ref, h2_ref, idx_ref, wt_ref):
    yb = _dot(o_ref[...], wn_ref[...])
    z = za_ref[...].astype(F32) + jax.nn.sigmoid(mb_ref[...].astype(F32)) * yb
    mix = _dot(z.astype(BF16), wo_ref[...])
    x1 = x_ref[...] + g1_ref[...] * mix
    x1_ref[...] = x1
    h2 = _modulated_rmsnorm(x1, n2_ref[...], sc_ref[...], sh_ref[...])
    h2_ref[...] = h2
    hi, lo = _split_bf16(h2)
    logits = _dot(hi, wrh_ref[...]) + (_dot(lo, wrh_ref[...]) + _dot(hi, wrl_ref[...]))
    idx, wts = _route(logits, rb_ref[...])
    idx_ref[...] = idx
    wt_ref[...] = wts


def out_proj(o, za, proj, x2, g1, norm_g, sc, sh, w_nsa_out, w_out, w_router, router_bias, seq, mb_col, tm=256):
    T, D = x2.shape
    E = w_router.shape[1]
    per_b = seq // tm
    wr_hi, wr_lo = _split_bf16(w_router)
    tok = lambda i: (i, 0)
    const = lambda i: (0, 0)
    per_batch = pl.BlockSpec((None, 1, D), lambda i: (i // per_b, 0, 0))
    return pl.pallas_call(
        _out_proj_kernel,
        grid=(T // tm,),
        in_specs=[pl.BlockSpec((tm, D), tok), pl.BlockSpec((tm, D), tok),
                  pl.BlockSpec((tm, D), lambda i: (i, mb_col // D)),
                  pl.BlockSpec((tm, D), tok), per_batch,
                  pl.BlockSpec((1, D), const), per_batch, per_batch,
                  pl.BlockSpec((D, D), const), pl.BlockSpec((D, D), const),
                  pl.BlockSpec((D, E), const), pl.BlockSpec((D, E), const), pl.BlockSpec((E, 1), const)],
        out_specs=[pl.BlockSpec((tm, D), tok), pl.BlockSpec((tm, D), tok),
                   pl.BlockSpec((tm, LANES), tok), pl.BlockSpec((tm, LANES), tok)],
        out_shape=[jax.ShapeDtypeStruct((T, D), F32), jax.ShapeDtypeStruct((T, D), F32),
                   jax.ShapeDtypeStruct((T, LANES), I32), jax.ShapeDtypeStruct((T, LANES), F32)],
        compiler_params=_params("arbitrary"),
        name="out_proj",
    )(o, za, proj, x2, g1[:, None, :], norm_g.reshape(1, D), sc[:, None, :], sh[:, None, :],
      w_nsa_out.astype(BF16), w_out.astype(BF16), wr_hi, wr_lo, router_bias.reshape(E, 1))


def _experts_kernel(ex_ref, ni_ref, gidx_hbm, sidx_hbm, h_hbm, wg0, wu0, wd0, wg1, wu1, wd1, y_hbm,
                    g_idx, s_idx, x_buf, y_buf, g_sem, s_sem, row_sem, out_sem, *, bm, dump_row):
    n = pl.program_id(0)
    n_items = ni_ref[0]
    w0 = 2 * n
    w1 = w0 + 1

    def idx_copy(src_hbm, w, smem, sem, s):
        return pltpu.make_async_copy(src_hbm.at[w], smem.at[s], sem.at[s])

    gather_idx = functools.partial(idx_copy, gidx_hbm, smem=g_idx, sem=g_sem)
    scatter_idx = functools.partial(idx_copy, sidx_hbm, smem=s_idx, sem=s_sem)

    def issue_gather(s):
        for r in range(bm):
            pltpu.make_async_copy(h_hbm.at[pl.ds(g_idx[s, r], 1), :], x_buf.at[s, pl.ds(r, 1), :],
                                  row_sem.at[s]).start(priority=1)

    def issue_scatter(s):
        for r in range(bm):
            pltpu.make_async_copy(y_buf.at[s, pl.ds(r, 1), :], y_hbm.at[pl.ds(s_idx[s, r], 1), :],
                                  out_sem.at[s]).start(priority=r % 2)

    def wait_gather(s):
        pltpu.make_async_copy(h_hbm.at[pl.ds(0, bm), :], x_buf.at[s], row_sem.at[s]).wait()

    def wait_scatter(s):
        pltpu.make_async_copy(y_buf.at[s], y_hbm.at[pl.ds(0, bm), :], out_sem.at[s]).wait()

    def expert(s, wg, wu, wd):
        x = x_buf[s].astype(BF16)
        gate = _dot(x, wg[...].astype(BF16))
        up = _dot(x, wu[...].astype(BF16))
        hid = gate * jax.nn.sigmoid(gate) * up
        y_buf[s] = _dot(hid.astype(BF16), wd[...].astype(BF16))

    @pl.when(n == 0)
    def _():
        y_buf[...] = jnp.zeros(y_buf.shape, F32)
        for s in range(2):
            pltpu.make_async_copy(y_buf.at[s], y_hbm.at[pl.ds(dump_row + s * bm, bm), :], out_sem.at[s]).start()
        gather_idx(w=0, s=0).start()
        gather_idx(w=1, s=1).start()
        gather_idx(w=0, s=0).wait()
        issue_gather(0)

    @pl.when(w0 < n_items)
    def _():
        gather_idx(w=w0 + 2, s=0).start()
        scatter_idx(w=w0, s=0).start()
        scatter_idx(w=w1, s=1).start()
        gather_idx(w=w1, s=1).wait()
        issue_gather(1)
        gather_idx(w=w1 + 2, s=1).start()
        wait_gather(0)
        wait_scatter(0)
        expert(0, wg0, wu0, wd0)
        scatter_idx(w=w0, s=0).wait()
        issue_scatter(0)
        gather_idx(w=w0 + 2, s=0).wait()
        issue_gather(0)
        wait_gather(1)
        wait_scatter(1)
        expert(1, wg1, wu1, wd1)
        scatter_idx(w=w1, s=1).wait()
        issue_scatter(1)

        @pl.when(w0 + 2 >= n_items)
        def _():
            wait_gather(0)
            gather_idx(w=w1 + 2, s=1).wait()
            wait_scatter(0)
            wait_scatter(1)


def routed_experts(h2, plan, w_gate, w_up, w_down, n_assign, bm=EXPERT_BLOCK):
    ex, n_items, gidx, sidx = plan
    T, D = h2.shape
    E, _, DE = w_gate.shape
    n_work = ex.shape[0] - 2
    even = lambda n, ex, ni: (ex[2 * n], 0, 0)
    odd = lambda n, ex, ni: (ex[2 * n + 1], 0, 0)
    grid_spec = pltpu.PrefetchScalarGridSpec(
        num_scalar_prefetch=2,
        grid=(n_work // 2,),
        in_specs=[pl.BlockSpec(memory_space=pl.ANY), pl.BlockSpec(memory_space=pl.ANY),
                  pl.BlockSpec(memory_space=pl.ANY),
                  pl.BlockSpec((None, D, DE), even), pl.BlockSpec((None, D, DE), even), pl.BlockSpec((None, DE, D), even),
                  pl.BlockSpec((None, D, DE), odd), pl.BlockSpec((None, D, DE), odd), pl.BlockSpec((None, DE, D), odd)],
        out_specs=pl.BlockSpec(memory_space=pl.ANY),
        scratch_shapes=[pltpu.SMEM((2, bm), I32), pltpu.SMEM((2, bm), I32),
                        pltpu.VMEM((2, bm, D), F32), pltpu.VMEM((2, bm, D), F32),
                        pltpu.SemaphoreType.DMA((2,)), pltpu.SemaphoreType.DMA((2,)),
                        pltpu.SemaphoreType.DMA((2,)), pltpu.SemaphoreType.DMA((2,))],
    )
    return pl.pallas_call(
        functools.partial(_experts_kernel, bm=bm, dump_row=n_assign),
        grid_spec=grid_spec,
        out_shape=jax.ShapeDtypeStruct((n_assign + 2 * bm, D), F32),
        compiler_params=_params("arbitrary"),
        name="experts",
    )(ex, n_items, gidx, sidx, h2, w_gate, w_up, w_down, w_gate, w_up, w_down)


def _combine_kernel(*refs):
    y8_refs = refs[:TOP_K]
    h2_ref, x1_ref, wt_ref, g2_ref, fg_ref, wsg_ref, wsu_ref, wsd_ref, o_ref = refs[TOP_K:]
    h2 = h2_ref[...].astype(BF16)
    gate = _dot(h2, wsg_ref[...])
    up = _dot(h2, wsu_ref[...])
    y = _dot((gate * jax.nn.sigmoid(gate) * up).astype(BF16), wsd_ref[...])
    wt = wt_ref[...]
    for k in range(TOP_K):
        y = y + wt[:, k:k + 1] * y8_refs[k][...]
    x2 = x1_ref[...] + g2_ref[...] * y
    o_ref[...] = x2 * lax.rsqrt(jnp.mean(x2 * x2, axis=-1, keepdims=True) + EPS) * fg_ref[...]


def combine(y8, h2, x1, wts, g2, final_g, w_sh_gate, w_sh_up, w_sh_down, seq, tj=COMBINE_TILE):
    T, D = x1.shape
    DS = w_sh_gate.shape[1]
    per_b = seq // tj
    nt = T // tj
    tok = lambda i: (i, 0)
    const = lambda i: (0, 0)
    slot = lambda k: pl.BlockSpec((tj, D), lambda i: (k * nt + i, 0))
    return pl.pallas_call(
        _combine_kernel,
        grid=(nt,),
        in_specs=[slot(k) for k in range(TOP_K)] + [
                  pl.BlockSpec((tj, D), tok), pl.BlockSpec((tj, D), tok), pl.BlockSpec((tj, LANES), tok),
                  pl.BlockSpec((None, 1, D), lambda i: (i // per_b, 0, 0)),
                  pl.BlockSpec((1, D), const),
                  pl.BlockSpec((D, DS), const), pl.BlockSpec((D, DS), const), pl.BlockSpec((DS, D), const)],
        out_specs=pl.BlockSpec((tj, D), tok),
        out_shape=jax.ShapeDtypeStruct((T, D), F32),
        compiler_params=_params("arbitrary"),
        name="combine",
    )(*([y8] * TOP_K), h2, x1, wts, g2[:, None, :], final_g.reshape(1, D),
      w_sh_gate.astype(BF16), w_sh_up.astype(BF16), w_sh_down.astype(BF16))


def _dispatch_plan(idx, T, bm):
    A = T * TOP_K
    assert A % bm == 0
    nblk = A // bm
    E = N_EXPERTS
    kbits = TOP_K.bit_length() - 1
    sorted_e, order = lax.sort((idx.reshape(A), jnp.arange(A, dtype=I32)), num_keys=1)
    bounds = jnp.searchsorted(sorted_e, jnp.arange(E + 1, dtype=I32), side='left').astype(I32)
    start, end = bounds[:-1], bounds[1:]
    first_blk = start // bm
    n_it = jnp.where(end > start, (end - 1) // bm - first_blk + 1, 0)
    it_end = jnp.cumsum(n_it)
    it_start = it_end - n_it
    n_items = it_end[-1]
    n_work = nblk + E
    w = jnp.arange(n_work + 2, dtype=I32)
    real = w < n_items
    wc = jnp.minimum(w, n_items - 1)
    ex = jnp.searchsorted(it_end, wc, side='right').astype(I32)
    blk = first_blk[ex] + wc - it_start[ex]
    lo = jnp.maximum(start[ex], blk * bm) - blk * bm
    hi = jnp.where(real, jnp.minimum(end[ex], (blk + 1) * bm) - blk * bm, lo)
    rows = order.reshape(nblk, bm)[blk]
    r = jnp.arange(bm, dtype=I32)[None, :]
    mine = (r >= lo[:, None]) & (r < hi[:, None])
    gidx = rows >> kbits
    sidx = jnp.where(mine, (rows & (TOP_K - 1)) * T + gidx, A + (w[:, None] % 2) * bm + r)
    return ex, n_items.astype(I32).reshape(1), gidx.astype(I32), sidx.astype(I32)


def _layer(x, c, w_ada, b_ada, norm1_g, w_in, b_in, conv_w, conv_b, conv_ln_g, conv_ln_b, w_conv_out, b_conv_out,
           cmp_pe_k, cmp_w1_k, cmp_w2_k, cmp_pe_v, cmp_w1_v, cmp_w2_v, w_nsa_out, w_out, norm2_g, w_router,
           router_bias, w_sh_gate, w_sh_up, w_sh_down, w_gate, w_up, w_down, final_g):
    B, S, D = x.shape
    T = B * S
    G, dh = N_KV_GROUPS, HEAD_DIM
    n_q = N_HEADS * dh
    n_kv = 3 * 2 * G * dh
    n_ng = 3 * N_HEADS
    mod = ada_modulation(c, w_ada, b_ada)
    sh1, sc1, g1, sh2, sc2, g2 = jnp.split(mod, 6, axis=-1)

    c_q = 2 * D_CONV
    c_kv = c_q + n_q
    c_ng = c_kv + n_kv
    c_mg = c_ng + n_ng
    pad = LANES - n_ng
    w_r = jnp.concatenate([w_in[:, :c_kv], w_in[:, c_mg:], w_in[:, c_kv:c_mg], jnp.zeros((D, pad), F32)], axis=1)
    b_r = jnp.concatenate([b_in[:c_kv], b_in[c_mg:], b_in[c_kv:c_mg], jnp.zeros((pad,), F32)])
    col_merge = c_kv
    col_kv = col_merge + 2 * D
    col_ng = col_kv + n_kv
    x2 = x.reshape(T, D)
    proj = in_proj(x2, norm1_g, sc1, sh1, w_r.astype(BF16), b_r, S)

    za = conformer_conv(proj, B, S, D, col_merge, conv_w, conv_b, conv_ln_g, conv_ln_b, w_conv_out, b_conv_out)

    kv = proj[:, col_kv:col_kv + n_kv].reshape(B, S, 3, 2, G, dh)
    NC = S // CMP_STRIDE
    zc = kv[:, :, 0].reshape(B, NC, CMP_STRIDE, 2, G, dh).transpose(0, 3, 4, 1, 2, 5).reshape(B, 2, G, NC, CMP_STRIDE * dh)
    kvc = compress_kv(zc, jnp.stack([cmp_pe_k, cmp_pe_v]), jnp.stack([cmp_w1_k, cmp_w1_v]),
                      jnp.stack([cmp_w2_k, cmp_w2_v]))

    cs = CMP_STRIDE * jnp.arange(NC)[:, None]
    ss = SEL_BLOCK * jnp.arange(LANES)[None, :]
    overlap = ((cs <= ss + SEL_BLOCK - 1) & (cs + CMP_BLOCK - 1 >= ss) & (jnp.arange(NC)[:, None] < NC - 1)
               & (jnp.arange(LANES)[None, :] < S // SEL_BLOCK)).astype(BF16)
    oc, selbias = cmp_attention(proj, kvc[:, 0].transpose(0, 1, 3, 2), kvc[:, 1], overlap, B, S, c_q)

    keys_t = lambda br: kv[:, :, br, 0].transpose(0, 2, 3, 1)
    ones_col = jnp.concatenate([jnp.ones((B, G, S, 1), BF16), jnp.zeros((B, G, S, LANES - dh - 1), BF16)], axis=-1)
    vals_aug = lambda br: jnp.concatenate([kv[:, :, br, 1].transpose(0, 2, 1, 3), ones_col], axis=-1)
    onehot_t = (jnp.arange(LANES)[:, None] == jnp.arange(S)[None, :] // SEL_BLOCK).astype(BF16)
    ksa_t = jnp.concatenate([jnp.broadcast_to(onehot_t, (B, G, LANES, S)), keys_t(1)], axis=2)
    o = nsa_attention(proj, selbias, ksa_t, vals_aug(1), keys_t(2), vals_aug(2), oc, B, S, c_q, col_ng)

    x1, h2, idx, wts = out_proj(o, za, proj, x2, g1, norm2_g, sc2, sh2, w_nsa_out, w_out, w_router, router_bias,
                                S, col_merge + D)

    plan = _dispatch_plan(idx[:, :TOP_K], T, EXPERT_BLOCK)
    y8 = routed_experts(h2, plan, w_gate, w_up, w_down, T * TOP_K)
    out = combine(y8, h2, x1, wts, g2, final_g, w_sh_gate, w_sh_up, w_sh_down, S)
    return out.reshape(B, S, D)


def kernel(x, c, w_ada, b_ada, norm1_g, w_in, b_in, conv_w, conv_b, conv_ln_g, conv_ln_b, w_conv_out, b_conv_out, cmp_pe_k, cmp_w1_k, cmp_w2_k, cmp_pe_v, cmp_w1_v, cmp_w2_v, w_nsa_out, w_out, norm2_g, w_router, router_bias, w_sh_gate, w_sh_up, w_sh_down, w_gate, w_up, w_down, final_g):
    assert w_ada.shape[0] == 1, "single-layer block"
    return _layer(x, c, w_ada[0], b_ada[0], norm1_g[0], w_in[0], b_in[0], conv_w[0], conv_b[0], conv_ln_g[0],
                  conv_ln_b[0], w_conv_out[0], b_conv_out[0], cmp_pe_k[0], cmp_w1_k[0], cmp_w2_k[0], cmp_pe_v[0],
                  cmp_w1_v[0], cmp_w2_v[0], w_nsa_out[0], w_out[0], norm2_g[0], w_router[0], router_bias[0],
                  w_sh_gate[0], w_sh_up[0], w_sh_down[0], w_gate[0], w_up[0], w_down[0], final_g)
```
